```python
import math
import jax, jax.numpy as jnp
from jax import lax
import numpy as np

D_MODEL = 1024
BATCH = 8
SEQ = 2048
DEPTH = 1
DEC_BATCH = 128
DEC_SEQ = 1
PAST_LEN = 8192
PAGE_SIZE = 128

N_HEADS = 8
HEAD_DIM = 64
ATT_WIDTH = N_HEADS * HEAD_DIM
BLOCK = 256
TOP_K = 3
Q_BLOCK = 128
ROPE_THETA = 10000.0
CHUNK = 128
N_GROUPS = 8
GMLP_WIDTH = 512
GROUP_WIDTH = GMLP_WIDTH // N_GROUPS
D_FF = 2816
PROJ_WIDTH = 2 * GMLP_WIDTH + 3 * ATT_WIDTH + 2 * D_MODEL
ALPHA = (2 * DEPTH) ** 0.25
BETA = (8 * DEPTH) ** -0.25
LN_EPS = 1e-5
NEG_INF = -1e30

kernel_name = "hybrid_gmlp_moba_decoder_step"


def layer_norm(x, g, b):
    xf = x.astype(jnp.float32)
    mu = jnp.mean(xf, -1, keepdims=True)
    var = jnp.mean(jnp.square(xf - mu), -1, keepdims=True)
    return ((xf - mu) * lax.rsqrt(var + LN_EPS) * g.astype(jnp.float32) + b.astype(jnp.float32)).astype(x.dtype)


def rope(x, pos):
    half = HEAD_DIM // 2
    inv = ROPE_THETA ** (-jnp.arange(half, dtype=jnp.float32) / half)
    ang = pos.astype(jnp.float32)[:, None] * inv[None, :]
    cos = jnp.cos(ang)[None, :, None, :]
    sin = jnp.sin(ang)[None, :, None, :]
    xf = x.astype(jnp.float32)
    x1, x2 = xf[..., :half], xf[..., half:]
    return jnp.concatenate([x1 * cos - x2 * sin, x2 * cos + x1 * sin], -1).astype(x.dtype)


def swiglu_ffn(x, w_in, w_out):
    a, g = jnp.split(x @ w_in, 2, axis=-1)
    return (jax.nn.silu(g) * a) @ w_out


def chunk_spatial_gate(u, v, w_s, b_s):
    bn, L, _ = v.shape
    c = min(CHUNK, L)
    n = -(-L // c)
    pad = n * c - L
    vp = jnp.pad(v, ((0, 0), (0, pad), (0, 0))).reshape(bn, n, c, N_GROUPS, GROUP_WIDTH)
    mask = jnp.tril(jnp.ones((c, c), dtype=bool))
    w = jnp.where(mask[None], w_s[:, :c, :c], 0).astype(v.dtype)
    mixed = jnp.einsum('gts,bnsgc->bntgc', w, vp) + b_s[:, :c].T[:, :, None].astype(v.dtype)
    mixed = mixed.reshape(bn, n * c, GMLP_WIDTH)[:, :L]
    return u * mixed


def moba_select(q, k_mean, q_blk):
    s = jnp.einsum('bthd,bjhd->bhtj', q.astype(jnp.float32), k_mean.astype(jnp.float32))
    nb = k_mean.shape[1]
    s = jnp.where(jnp.arange(nb)[None, None, None, :] < q_blk[None, None, :, None], s, NEG_INF)
    n_sel = max(1, min(TOP_K, nb))
    _, idx = lax.top_k(s, n_sel)
    own = jnp.broadcast_to(q_blk[None, None, :, None], idx.shape[:3] + (1,))
    return jnp.concatenate([idx.astype(jnp.int32), own.astype(jnp.int32)], -1)


def moba_attend(q, q_pos, k_g, v_g, key_pos, n_past):
    n_sel = k_g.shape[3] - 1
    s = jnp.einsum('bhtd,bhtnkd->bhtnk', q.astype(jnp.float32), k_g.astype(jnp.float32)) / math.sqrt(HEAD_DIM)
    rank = jnp.arange(n_sel + 1)
    past_ok = rank[None, :] < n_past[:, None]
    causal = key_pos <= q_pos[:, None, None]
    valid = jnp.where((rank == n_sel)[:, None], causal, past_ok[:, :, None])
    s = jnp.where(valid, s, NEG_INF)
    b, h, t, n, k = s.shape
    p = jax.nn.softmax(s.reshape(b, h, t, n * k), axis=-1).reshape(b, h, t, n, k)
    out = jnp.einsum('bhtnk,bhtnkd->bhtd', p, v_g.astype(jnp.float32))
    return out.astype(q.dtype)


def moba_prompt(q, k, v):
    B, S, H, D = q.shape
    nb = -(-S // BLOCK)
    pad = nb * BLOCK - S
    kp = jnp.pad(k, ((0, 0), (0, pad), (0, 0), (0, 0))).reshape(B, nb, BLOCK, H, D)
    vp = jnp.pad(v, ((0, 0), (0, pad), (0, 0), (0, 0))).reshape(B, nb, BLOCK, H, D)
    k_mean = jnp.mean(kp.astype(jnp.float32), axis=2)
    pos = jnp.arange(S)
    q_blk = pos // BLOCK
    idx = moba_select(q, k_mean, q_blk)
    kb = kp.transpose(0, 3, 1, 2, 4)
    vb = vp.transpose(0, 3, 1, 2, 4)
    n_qc = S // Q_BLOCK
    qc = q.transpose(0, 2, 1, 3).reshape(B, H, n_qc, Q_BLOCK, D).transpose(2, 0, 1, 3, 4)
    ic = idx.reshape(B, H, n_qc, Q_BLOCK, -1).transpose(2, 0, 1, 3, 4)
    pc = pos.reshape(n_qc, Q_BLOCK)
    bi = jnp.arange(B)[:, None, None, None]
    hi = jnp.arange(H)[None, :, None, None]

    def one_query_block(args):
        qq, ii, pp = args
        k_g = kb[bi, hi, ii]
        v_g = vb[bi, hi, ii]
        key_pos = ii[..., None] * BLOCK + jnp.arange(BLOCK)
        return moba_attend(qq, pp, k_g, v_g, key_pos, pp // BLOCK)

    out = lax.map(one_query_block, (qc, ic, pc))
    return out.transpose(1, 0, 3, 2, 4).reshape(B, S, H * D)


def moba_sample(q, k_new, v_new, cache_k, cache_v, page_table):
    DB, T, H, D = q.shape
    n_pages = page_table.shape[1]
    page_size = cache_k.shape[1]
    past_len = n_pages * page_size
    pos = past_len + jnp.arange(T)
    q_blk = pos // BLOCK
    nb = -(-(past_len + T) // BLOCK)
    page_sum = jnp.sum(cache_k, axis=1, dtype=jnp.float32)[page_table]
    page_blk = jnp.arange(n_pages) * page_size // BLOCK
    blk_sum = jnp.zeros((DB, nb, H, D), jnp.float32).at[:, page_blk].add(page_sum)
    blk_sum = blk_sum.at[:, q_blk].add(k_new.astype(jnp.float32))
    k_mean = blk_sum / BLOCK
    idx = moba_select(q, k_mean, q_blk)
    key_pos = idx[..., None] * BLOCK + jnp.arange(BLOCK)
    bi = jnp.arange(DB)[:, None, None, None, None]
    hi = jnp.arange(H)[None, :, None, None, None]
    in_cache = (key_pos < past_len)[..., None]
    phys = page_table[bi, jnp.clip(key_pos // page_size, 0, n_pages - 1)]
    off = key_pos % page_size
    new_i = jnp.clip(key_pos - past_len, 0, T - 1)
    k_g = jnp.where(in_cache, cache_k[phys, off, hi], k_new[bi, new_i, hi])
    v_g = jnp.where(in_cache, cache_v[phys, off, hi], v_new[bi, new_i, hi])
    out = moba_attend(q.transpose(0, 2, 1, 3), pos, k_g, v_g, key_pos, q_blk)
    return out.transpose(0, 2, 1, 3).reshape(DB, T, H * D)


def mixer_projections(x, w_in, ln_v_g, ln_v_b, pos):
    h = x @ w_in
    s1 = 2 * GMLP_WIDTH
    h_uv, q, k, v, h_gate = jnp.split(h, [s1, s1 + ATT_WIDTH, s1 + 2 * ATT_WIDTH, s1 + 3 * ATT_WIDTH], axis=-1)
    u, vg = jnp.split(jax.nn.gelu(h_uv), 2, axis=-1)
    vg = layer_norm(vg, ln_v_g, ln_v_b)
    b, t, _ = x.shape
    q = rope(q.reshape(b, t, N_HEADS, HEAD_DIM), pos)
    k = rope(k.reshape(b, t, N_HEADS, HEAD_DIM), pos)
    v = v.reshape(b, t, N_HEADS, HEAD_DIM)
    return u, vg, q, k, v, h_gate


def merge_branches(y_a, y_b, h_gate, b_gate, w_proj_a, w_proj_b, w_o):
    gates = jax.nn.sigmoid((h_gate + b_gate).astype(jnp.float32)).astype(y_a.dtype)
    g_a, g_b = jnp.split(gates, 2, axis=-1)
    return (g_a * (y_a @ w_proj_a) + g_b * (y_b @ w_proj_b)) @ w_o


def setup_inputs(seed: int = 0) -> dict:
    key = jax.random.key(seed)
    ks = jax.random.split(key, 32)
    f32 = jnp.float32
    n_pages = PAST_LEN // PAGE_SIZE
    n_phys = (DEC_BATCH * n_pages * 5) // 4
    nrm = lambda k, shape, s: jax.random.normal(k, shape, f32) * s
    gain = lambda k, n: 1.0 + 0.02 * jax.random.normal(k, (DEPTH, n), f32)
    page_table = jax.random.permutation(ks[4], n_phys)[: DEC_BATCH * n_pages].reshape(DEC_BATCH, n_pages).astype(jnp.int32)
    return {
        "x_prompt": nrm(ks[0], (BATCH, SEQ, D_MODEL), 1.0),
        "x_sample": nrm(ks[1], (DEC_BATCH, DEC_SEQ, D_MODEL), 1.0),
        "cache_k": nrm(ks[2], (DEPTH, n_phys, PAGE_SIZE, N_HEADS, HEAD_DIM), 1.0),
        "cache_v": nrm(ks[3], (DEPTH, n_phys, PAGE_SIZE, N_HEADS, HEAD_DIM), 1.0),
        "page_table": page_table,
        "w_ffn1_in": nrm(ks[5], (DEPTH, D_MODEL, 2 * D_FF), D_MODEL ** -0.5),
        "w_ffn1_out": nrm(ks[6], (DEPTH, D_FF, D_MODEL), BETA * D_FF ** -0.5),
        "ln1_g": gain(ks[7], D_MODEL),
        "ln1_b": nrm(ks[8], (DEPTH, D_MODEL), 0.02),
        "w_in": nrm(ks[9], (DEPTH, D_MODEL, PROJ_WIDTH), D_MODEL ** -0.5),
        "w_s": nrm(ks[10], (DEPTH, N_GROUPS, CHUNK, CHUNK), CHUNK ** -0.5),
        "b_s": 1.0 + 0.02 * jax.random.normal(ks[11], (DEPTH, N_GROUPS, CHUNK), f32),
        "ln_v_g": gain(ks[12], GMLP_WIDTH),
        "ln_v_b": nrm(ks[13], (DEPTH, GMLP_WIDTH), 0.02),
        "b_gate": nrm(ks[14], (DEPTH, 2 * D_MODEL), 0.02),
        "w_proj_a": nrm(ks[15], (DEPTH, GMLP_WIDTH, D_MODEL), BETA * GMLP_WIDTH ** -0.5),
        "w_proj_b": nrm(ks[16], (DEPTH, ATT_WIDTH, D_MODEL), BETA * ATT_WIDTH ** -0.5),
        "w_o": nrm(ks[17], (DEPTH, D_MODEL, D_MODEL), BETA * D_MODEL ** -0.5),
        "ln2_g": gain(ks[18], D_MODEL),
        "ln2_b": nrm(ks[19], (DEPTH, D_MODEL), 0.02),
        "w_ffn2_in": nrm(ks[20], (DEPTH, D_MODEL, 2 * D_FF), D_MODEL ** -0.5),
        "w_ffn2_out": nrm(ks[21], (DEPTH, D_FF, D_MODEL), BETA * D_FF ** -0.5),
        "ln3_g": gain(ks[22], D_MODEL),
        "ln3_b": nrm(ks[23], (DEPTH, D_MODEL), 0.02),
    }


def reference(x_prompt, x_sample, cache_k, cache_v, page_table, w_ffn1_in, w_ffn1_out, ln1_g, ln1_b,
              w_in, w_s, b_s, ln_v_g, ln_v_b, b_gate, w_proj_a, w_proj_b, w_o, ln2_g, ln2_b,
              w_ffn2_in, w_ffn2_out, ln3_g, ln3_b):
    S = x_prompt.shape[1]
    T = x_sample.shape[1]
    past_len = page_table.shape[1] * cache_k.shape[2]
    pos_p = jnp.arange(S)
    pos_s = past_len + jnp.arange(T)
    xp, xs = x_prompt, x_sample
    k_p, v_p, k_s, v_s, cv_s = [], [], [], [], []
    for l in range(DEPTH):
        xp = layer_norm(ALPHA * xp + 0.5 * swiglu_ffn(xp, w_ffn1_in[l], w_ffn1_out[l]), ln1_g[l], ln1_b[l])
        xs = layer_norm(ALPHA * xs + 0.5 * swiglu_ffn(xs, w_ffn1_in[l], w_ffn1_out[l]), ln1_g[l], ln1_b[l])
        u, vg, q, k, v, hg = mixer_projections(xp, w_in[l], ln_v_g[l], ln_v_b[l], pos_p)
        y_a = chunk_spatial_gate(u, vg, w_s[l], b_s[l])
        y_b = moba_prompt(q, k, v)
        mix = merge_branches(y_a, y_b, hg, b_gate[l], w_proj_a[l], w_proj_b[l], w_o[l])
        xp = layer_norm(ALPHA * xp + mix, ln2_g[l], ln2_b[l])
        k_p.append(k)
        v_p.append(v)
        u, vg, q, k, v, hg = mixer_projections(xs, w_in[l], ln_v_g[l], ln_v_b[l], pos_s)
        y_a = chunk_spatial_gate(u, vg, w_s[l], b_s[l])
        y_b = moba_sample(q, k, v, cache_k[l], cache_v[l], page_table)
        mix = merge_branches(y_a, y_b, hg, b_gate[l], w_proj_a[l], w_proj_b[l], w_o[l])
        xs = layer_norm(ALPHA * xs + mix, ln2_g[l], ln2_b[l])
        k_s.append(k)
        v_s.append(v)
        cv_s.append(vg)
        xp = layer_norm(ALPHA * xp + 0.5 * swiglu_ffn(xp, w_ffn2_in[l], w_ffn2_out[l]), ln3_g[l], ln3_b[l])
        xs = layer_norm(ALPHA * xs + 0.5 * swiglu_ffn(xs, w_ffn2_in[l], w_ffn2_out[l]), ln3_g[l], ln3_b[l])
    return (xp, xs, jnp.stack(k_p), jnp.stack(v_p), jnp.stack(k_s), jnp.stack(v_s), jnp.stack(cv_s))
```

```python
import functools

import jax
import jax.numpy as jnp
from jax import lax
from jax.experimental import pallas as pl
from jax.experimental.pallas import tpu as pltpu

F32 = jnp.float32
BF16 = jnp.bfloat16

N_HEADS = 8
HEAD_DIM = 64
ATT_WIDTH = N_HEADS * HEAD_DIM
BLOCK = 256
TOP_K = 3
ROPE_THETA = 10000.0
CHUNK = 128
N_GROUPS = 8
GMLP_WIDTH = 512
LN_EPS = 1e-5
NEG_INF = -1e30
LANES = 128
FF_CHUNK = 256
VMEM_LIMIT = 56 * 1024 * 1024


def _resident(shape):
    zeros = (0,) * len(shape)
    return pl.BlockSpec(shape, lambda *_: zeros, pipeline_mode=pl.Buffered(1))


def _params(sem):
    return pltpu.CompilerParams(dimension_semantics=sem, vmem_limit_bytes=VMEM_LIMIT)


def _ln(y, g, b):
    mu = jnp.mean(y, axis=-1, keepdims=True)
    d = y - mu
    var = jnp.mean(d * d, axis=-1, keepdims=True)
    return d * lax.rsqrt(var + LN_EPS) * g + b


def _dot(a, b):
    return jnp.dot(a, b, preferred_element_type=F32)


def _dot_nt(a, b):
    return lax.dot_general(a, b, (((1,), (1,)), ((), ())), preferred_element_type=F32)


def _swiglu(xb, win_ref, wout_ref):
    d_ff = wout_ref.shape[0]
    acc = None
    for c in range(d_ff // FF_CHUNK):
        lo, hi = c * FF_CHUNK, (c + 1) * FF_CHUNK
        a = _dot(xb, win_ref[:, lo:hi])
        g = _dot(xb, win_ref[:, d_ff + lo:d_ff + hi])
        act = (g * jax.nn.sigmoid(g) * a).astype(BF16)
        part = _dot(act, wout_ref[lo:hi, :])
        acc = part if acc is None else acc + part
    return acc


def _ffn_ln_kernel(x_ref, win_ref, wout_ref, g_ref, b_ref, o_ref, *, alpha):
    x = x_ref[...]
    h = _swiglu(x.astype(BF16), win_ref, wout_ref)
    o_ref[...] = _ln(alpha * x + 0.5 * h, g_ref[...], b_ref[...])


def _ffn_ln(x, win, wout, g, b, *, alpha, tm):
    n, d = x.shape
    row = lambda i: (i, 0)
    return pl.pallas_call(
        functools.partial(_ffn_ln_kernel, alpha=alpha),
        grid=(n // tm,),
        in_specs=[pl.BlockSpec((tm, d), row), _resident(win.shape), _resident(wout.shape),
                  _resident(g.shape), _resident(b.shape)],
        out_specs=pl.BlockSpec((tm, d), row),
        out_shape=jax.ShapeDtypeStruct((n, d), F32),
        compiler_params=_params(("parallel",)),
        name="ffn_ln",
    )(x, win, wout, g, b)


def _rope(x, cos, sin_signed):
    lane = lax.broadcasted_iota(jnp.int32, x.shape, 1)
    first = (lane % HEAD_DIM) < (HEAD_DIM // 2)
    half = HEAD_DIM // 2
    partner = jnp.where(first, pltpu.roll(x, x.shape[1] - half, 1), pltpu.roll(x, half, 1))
    return x * cos + partner * sin_signed


def _mixer_kernel(x_ref, w_ref, lng_ref, lnb_ref, cos_ref, sin_ref, ws_ref, bs_ref, *out_refs, spatial):
    if spatial:
        ya_ref, q_ref, k_ref, v_ref, hg_ref = out_refs
    else:
        ya_ref, q_ref, k_ref, v_ref, hg_ref, vg_ref = out_refs
    xb = x_ref[...].astype(BF16)
    gw, aw = GMLP_WIDTH, ATT_WIDTH

    def proj(lo, hi):
        return _dot(xb, w_ref[:, lo:hi])

    u = jax.nn.gelu(proj(0, gw))
    vg = _ln(jax.nn.gelu(proj(gw, 2 * gw)), lng_ref[...], lnb_ref[...])
    if spatial:
        tm = x_ref.shape[0]
        r = lax.broadcasted_iota(jnp.int32, (CHUNK, CHUNK), 0)
        c = lax.broadcasted_iota(jnp.int32, (CHUNK, CHUNK), 1)
        tril = r >= c
        lane = lax.broadcasted_iota(jnp.int32, (CHUNK, LANES), 1)
        lo_half = lane < (LANES // 2)
        for p in range(N_GROUPS // 2):
            wcat = jnp.concatenate(
                [jnp.where(tril, ws_ref[2 * p], 0.0), jnp.where(tril, ws_ref[2 * p + 1], 0.0)],
                axis=1).astype(BF16)
            cols = slice(p * LANES, (p + 1) * LANES)
            for ch in range(tm // CHUNK):
                rows = slice(ch * CHUNK, (ch + 1) * CHUNK)
                vp = vg[rows, cols]
                vstack = jnp.concatenate(
                    [jnp.where(lo_half, vp, 0.0), jnp.where(lo_half, 0.0, vp)], axis=0).astype(BF16)
                mixed = _dot(wcat, vstack) + bs_ref[:, cols]
                ya_ref[rows, cols] = (u[rows, cols] * mixed).astype(ya_ref.dtype)
    else:
        ya_ref[...] = (u * (vg * ws_ref[...] + bs_ref[...])).astype(ya_ref.dtype)
        vg_ref[...] = vg
    cos, sin = cos_ref[...], sin_ref[...]
    q_ref[...] = _rope(proj(2 * gw, 2 * gw + aw), cos, sin)
    k_ref[...] = _rope(proj(2 * gw + aw, 2 * gw + 2 * aw), cos, sin)
    v_ref[...] = proj(2 * gw + 2 * aw, 2 * gw + 3 * aw)
    hg_ref[...] = proj(2 * gw + 3 * aw, w_ref.shape[1])


def _mixer(x, w, lng, lnb, cos, sin, ws, bs, *, spatial, tm):
    n, d = x.shape
    gate_w = w.shape[1] - 2 * GMLP_WIDTH - 3 * ATT_WIDTH
    row = lambda i: (i, 0)
    if spatial:
        n_tab = cos.shape[0] // tm
        tab = pl.BlockSpec((tm, ATT_WIDTH), lambda i: (i % n_tab, 0))
    else:
        tab = _resident(cos.shape)
    outs = [(GMLP_WIDTH, BF16), (ATT_WIDTH, F32), (ATT_WIDTH, F32), (ATT_WIDTH, F32), (gate_w, F32)]
    if not spatial:
        outs.append((GMLP_WIDTH, F32))
    return pl.pallas_call(
        functools.partial(_mixer_kernel, spatial=spatial),
        grid=(n // tm,),
        in_specs=[pl.BlockSpec((tm, d), row), _resident(w.shape), _resident(lng.shape), _resident(lnb.shape),
                  tab, tab, _resident(ws.shape), _resident(bs.shape)],
        out_specs=[pl.BlockSpec((tm, wd), row) for wd, _ in outs],
        out_shape=[jax.ShapeDtypeStruct((n, wd), dt) for wd, dt in outs],
        compiler_params=_params(("parallel",)),
        name="mixer_prompt" if spatial else "mixer_sample",
    )(x, w, lng, lnb, cos, sin, ws, bs)


def _moba_prompt_kernel(q_ref, k_ref, v_ref, o_ref, kme_ref, *, nb):
    qb = pl.program_id(1)
    nh = N_HEADS
    width = q_ref.shape[1]

    @pl.when(qb == 0)
    def _():
        means = [jnp.sum(k_ref[j * BLOCK:(j + 1) * BLOCK, :], axis=0, keepdims=True) for j in range(nb)]
        km = jnp.concatenate(means, axis=0) * (1.0 / BLOCK)
        kme = jnp.broadcast_to(km[:, None, :], (nb, nh, width)).reshape(nb * nh, width)
        c = lax.broadcasted_iota(jnp.int32, (nb * nh, width), 0)
        l = lax.broadcasted_iota(jnp.int32, (nb * nh, width), 1)
        kme = jnp.where((l // HEAD_DIM) == (c % nh), kme, 0.0)
        kme_ref[...] = jnp.concatenate([kme, kme], axis=0)

    q = q_ref[...]
    s = lax.dot_general(q, kme_ref[...], (((1,), (1,)), ((), ())),
                        precision=lax.Precision.HIGHEST, preferred_element_type=F32)
    lane = lax.broadcasted_iota(jnp.int32, s.shape, 1)
    jb = (lane % (nb * nh)) // nh
    s = jnp.where(jb < qb, s, NEG_INF)
    rank = jnp.zeros(s.shape, jnp.int32)
    for r in range(1, nb):
        sr = pltpu.roll(s, nh * r, 1)
        jr = (jb + (nb - r)) % nb
        ahead = (sr > s) | ((sr == s) & (jr < jb))
        rank = rank + ahead.astype(jnp.int32)
    self = jnp.where((rank < min(TOP_K, nb)) & (jb < qb) & (lane < nb * nh), 1.0, 0.0)

    lane_p = lax.broadcasted_iota(jnp.int32, (BLOCK, LANES), 1)
    lo_half = lane_p < HEAD_DIM
    rr = lax.broadcasted_iota(jnp.int32, (BLOCK, BLOCK), 0)
    cc = lax.broadcasted_iota(jnp.int32, (BLOCK, BLOCK), 1)
    causal = rr >= cc
    own = pl.ds(pl.multiple_of(qb * BLOCK, BLOCK), BLOCK)

    for hp in range(nh // 2):
        cols = slice(hp * LANES, (hp + 1) * LANES)
        qp = q[:, cols] * (HEAD_DIM ** -0.5)
        q_a = jnp.where(lo_half, qp, 0.0).astype(BF16)
        q_b = jnp.where(lo_half, 0.0, qp).astype(BF16)
        k_own = k_ref[own, cols].astype(BF16)
        v_own = v_ref[own, cols].astype(BF16)

        def first(qx):
            sc = jnp.where(causal, _dot_nt(qx, k_own), NEG_INF)
            m = jnp.max(sc, axis=1, keepdims=True)
            p = jnp.exp(sc - m)
            return m, jnp.sum(p, axis=1, keepdims=True), _dot(p.astype(BF16), v_own)

        def step(j, carry):
            rows = pl.ds(pl.multiple_of(j * BLOCK, BLOCK), BLOCK)
            k_j = k_ref[rows, cols].astype(BF16)
            v_j = v_ref[rows, cols].astype(BF16)
            new = []
            for t, qx in enumerate((q_a, q_b)):
                m, l, acc = carry[3 * t:3 * t + 3]
                picked = jnp.sum(jnp.where(lane_p == j * nh + 2 * hp + t, self, 0.0), axis=1, keepdims=True)
                sc = jnp.where(picked > 0.0, _dot_nt(qx, k_j), NEG_INF)
                m_new = jnp.maximum(m, jnp.max(sc, axis=1, keepdims=True))
                scale = jnp.exp(m - m_new)
                p = jnp.exp(sc - m_new)
                new += [m_new, scale * l + jnp.sum(p, axis=1, keepdims=True),
                        scale * acc + _dot(p.astype(BF16), v_j)]
            return tuple(new)

        m_a, l_a, acc_a, m_b, l_b, acc_b = lax.fori_loop(0, qb, step, first(q_a) + first(q_b))
        o_ref[:, cols] = jnp.where(lo_half, acc_a / l_a, acc_b / l_b).astype(o_ref.dtype)


def _moba_prompt(q, k, v, *, batch, seq):
    nb = seq // BLOCK
    assert nb * BLOCK == seq and 2 * nb * N_HEADS == LANES, "prompt MoBA sweep expects 8 key blocks of 256"
    width = q.shape[1]
    return pl.pallas_call(
        functools.partial(_moba_prompt_kernel, nb=nb),
        grid=(batch, nb),
        in_specs=[pl.BlockSpec((BLOCK, width), lambda b, i: (b * nb + i, 0)),
                  pl.BlockSpec((seq, width), lambda b, i: (b, 0)),
                  pl.BlockSpec((seq, width), lambda b, i: (b, 0))],
        out_specs=pl.BlockSpec((BLOCK, width), lambda b, i: (b * nb + i, 0)),
        out_shape=jax.ShapeDtypeStruct(q.shape, BF16),
        scratch_shapes=[pltpu.VMEM((2 * nb * N_HEADS, width), F32)],
        compiler_params=_params(("parallel", "arbitrary")),
        name="moba_prompt",
    )(q, k, v)


PAGES_PER_STEP = 16


def _column_broadcast(row):
    return jnp.transpose(jnp.broadcast_to(row, (LANES, row.shape[1])))


def _per_head_sum(x):
    return jnp.sum(x.reshape(N_HEADS, HEAD_DIM, x.shape[1]), axis=1)


def _logits_kernel(pt_ref, q_ref, *refs, ppb):
    del pt_ref
    n_in = len(refs) - 4
    pages = refs[:n_in]
    lg_ref, idx_ref, qcol_ref, bsc_ref = refs[n_in:]
    c = pl.program_id(1)
    rows = pages[0].shape[-1]
    blocks_per_step = n_in // ppb

    @pl.when(c == 0)
    def _():
        qcol_ref[...] = _column_broadcast(q_ref[0] * (HEAD_DIM ** -0.5))

    qcol = qcol_ref[...]
    for j in range(blocks_per_step):
        tot = None
        for p in range(ppb):
            i = j * ppb + p
            kt = pages[i][0, 0].reshape(N_HEADS * HEAD_DIM, rows)
            lg = _per_head_sum(kt * qcol)
            lg_ref[0, :, i * rows:(i + 1) * rows] = lg
            tot = lg if tot is None else tot + lg
        bsc_ref[c * blocks_per_step + j] = jnp.sum(tot, axis=1, keepdims=True)

    @pl.when(c == pl.num_programs(1) - 1)
    def _():
        sc = bsc_ref[...] * (HEAD_DIM ** 0.5 / BLOCK)
        nblk = sc.shape[0]
        jio = lax.broadcasted_iota(jnp.int32, sc.shape, 0)
        picks = []
        for _ in range(TOP_K):
            mx = jnp.max(sc, axis=0, keepdims=True)
            ix = jnp.min(jnp.where(sc == mx, jio, nblk), axis=0, keepdims=True)
            picks.append(ix)
            sc = jnp.where(jio == ix, -jnp.inf, sc)
        idx_ref[0] = jnp.concatenate(picks, axis=0)


def _sample_logits(cache_kt, page_table, q, layer):
    _, _, nh, hd, rows = cache_kt.shape
    db, n_pages = page_table.shape
    ppb = BLOCK // rows
    nblk = n_pages // ppb
    assert nblk >= TOP_K and n_pages % PAGES_PER_STEP == 0
    steps = n_pages // PAGES_PER_STEP

    def page_spec(i):
        return pl.BlockSpec((1, 1, nh, hd, rows),
                            lambda b, c, pt: (layer, pt[b, c * PAGES_PER_STEP + i], 0, 0, 0))

    return pl.pallas_call(
        functools.partial(_logits_kernel, ppb=ppb),
        grid_spec=pltpu.PrefetchScalarGridSpec(
            num_scalar_prefetch=1,
            grid=(db, steps),
            in_specs=[pl.BlockSpec((1, 1, nh * hd), lambda b, c, pt: (b, 0, 0))]
                     + [page_spec(i) for i in range(PAGES_PER_STEP)],
            out_specs=[pl.BlockSpec((1, nh, PAGES_PER_STEP * rows), lambda b, c, pt: (b, 0, c)),
                       pl.BlockSpec((1, TOP_K, nh, 1), lambda b, c, pt: (b, 0, 0, 0))],
            scratch_shapes=[pltpu.VMEM((nh * hd, LANES), F32), pltpu.VMEM((nblk, nh, 1), F32)],
        ),
        out_shape=[jax.ShapeDtypeStruct((db, nh, n_pages * rows), F32),
                   jax.ShapeDtypeStruct((db, TOP_K, nh, 1), jnp.int32)],
        compiler_params=_params(("parallel", "arbitrary")),
        name="sample_logits",
    )(page_table, q, *([cache_kt] * PAGES_PER_STEP))


def _moba_sample_kernel(pt_ref, idx_ref, q_ref, kn_ref, vn_ref, lg_ref, cv_ref, o_ref, vbuf, sem, *, layer, ppb):
    b = pl.program_id(0)
    rows = cv_ref.shape[-1]
    slabs = [(h, r, pg) for h in range(N_HEADS) for r in range(TOP_K) for pg in range(ppb)]

    def slab_copy(h, r, pg):
        blk = idx_ref[b, r * N_HEADS + h]
        phys = pt_ref[b, blk * ppb + pg]
        return pltpu.make_async_copy(cv_ref.at[layer, phys, h], vbuf.at[h, r * ppb + pg], sem.at[0])

    for s in slabs:
        slab_copy(*s).start()

    s_sel = []
    for r in range(TOP_K):
        per_head = []
        for h in range(N_HEADS):
            start = pl.multiple_of(idx_ref[b, r * N_HEADS + h] * BLOCK, BLOCK)
            per_head.append(lg_ref[0, h:h + 1, pl.ds(start, BLOCK)])
        s_sel.append(jnp.concatenate(per_head, axis=0))
    qcol = _column_broadcast(q_ref[0] * (HEAD_DIM ** -0.5))
    s_self = _per_head_sum(qcol * _column_broadcast(kn_ref[0]))[:, :1]
    m = s_self
    for s in s_sel:
        m = jnp.maximum(m, jnp.max(s, axis=1, keepdims=True))
    p_sel = [jnp.exp(s - m) for s in s_sel]
    p_self = jnp.exp(s_self - m)
    denom = p_self
    for p in p_sel:
        denom = denom + jnp.sum(p, axis=1, keepdims=True)

    for s in slabs:
        slab_copy(*s).wait()

    vn_col = _column_broadcast(vn_ref[0])[:, :1]
    out_cols = []
    for h in range(N_HEADS):
        acc = None
        for r in range(TOP_K):
            for pg in range(ppb):
                term = vbuf[h, r * ppb + pg] * p_sel[r][h:h + 1, pg * rows:(pg + 1) * rows]
                acc = term if acc is None else acc + term
        num = jnp.sum(acc, axis=1, keepdims=True) + p_self[h:h + 1, :] * vn_col[h * HEAD_DIM:(h + 1) * HEAD_DIM, :]
        out_cols.append(num / denom[h:h + 1, :])
    out_col = jnp.concatenate(out_cols, axis=0)
    o_ref[0] = jnp.transpose(jnp.broadcast_to(out_col, (out_col.shape[0], LANES)))[:1, :]


def _moba_sample(page_table, idx, q, k_new, v_new, logits, cache_vt, layer):
    db, _, width = q.shape
    _, _, nh, hd, rows = cache_vt.shape
    ppb = BLOCK // rows
    seq_spec = pl.BlockSpec((1, 1, width), lambda b, pt, ix: (b, 0, 0))
    return pl.pallas_call(
        functools.partial(_moba_sample_kernel, layer=layer, ppb=ppb),
        grid_spec=pltpu.PrefetchScalarGridSpec(
            num_scalar_prefetch=2,
            grid=(db,),
            in_specs=[seq_spec, seq_spec, seq_spec,
                      pl.BlockSpec((1, nh, logits.shape[2]), lambda b, pt, ix: (b, 0, 0)),
                      pl.BlockSpec(memory_space=pl.ANY)],
            out_specs=seq_spec,
            scratch_shapes=[pltpu.VMEM((nh, TOP_K * ppb, hd, rows), F32), pltpu.SemaphoreType.DMA((1,))],
        ),
        out_shape=jax.ShapeDtypeStruct((db, 1, width), F32),
        compiler_params=_params(("arbitrary",)),
        name="moba_sample",
    )(page_table, idx, q, k_new, v_new, logits, cache_vt)


def _merge_ffn_kernel(x_ref, ya_ref, yb_ref, hg_ref, bg_ref, wa_ref, wb_ref, wo_ref, g2_ref, b2_ref,
                      win_ref, wout_ref, g3_ref, b3_ref, o_ref, *, alpha):
    x = x_ref[...]
    dm = x.shape[1]
    gates = jax.nn.sigmoid(hg_ref[...] + bg_ref[...])
    mix_in = gates[:, :dm] * _dot(ya_ref[...], wa_ref[...]) + gates[:, dm:] * _dot(yb_ref[...], wb_ref[...])
    x2 = _ln(alpha * x + _dot(mix_in.astype(BF16), wo_ref[...]), g2_ref[...], b2_ref[...])
    h = _swiglu(x2.astype(BF16), win_ref, wout_ref)
    o_ref[...] = _ln(alpha * x2 + 0.5 * h, g3_ref[...], b3_ref[...])


def _merge_ffn(x, ya, yb, hg, bg, wa, wb, wo, g2, b2, win, wout, g3, b3, *, alpha, tm):
    n, d = x.shape
    row = lambda i: (i, 0)
    tiles = [x, ya, yb, hg]
    consts = [bg, wa, wb, wo, g2, b2, win, wout, g3, b3]
    return pl.pallas_call(
        functools.partial(_merge_ffn_kernel, alpha=alpha),
        grid=(n // tm,),
        in_specs=[pl.BlockSpec((tm, a.shape[1]), row) for a in tiles] + [_resident(a.shape) for a in consts],
        out_specs=pl.BlockSpec((tm, d), row),
        out_shape=jax.ShapeDtypeStruct((n, d), F32),
        compiler_params=_params(("parallel",)),
        name="merge_ffn",
    )(*tiles, *consts)


def _rope_tables(pos):
    half = HEAD_DIM // 2
    inv = ROPE_THETA ** (-jnp.arange(half, dtype=F32) / half)
    ang = pos.astype(F32)[:, None] * inv[None, :]
    cos, sin = jnp.cos(ang), jnp.sin(ang)
    return (jnp.tile(jnp.concatenate([cos, cos], -1), (1, N_HEADS)),
            jnp.tile(jnp.concatenate([-sin, sin], -1), (1, N_HEADS)))


def kernel(x_prompt, x_sample, cache_k, cache_v, page_table, w_ffn1_in, w_ffn1_out, ln1_g, ln1_b, w_in, w_s, b_s,
           ln_v_g, ln_v_b, b_gate, w_proj_a, w_proj_b, w_o, ln2_g, ln2_b, w_ffn2_in, w_ffn2_out, ln3_g, ln3_b):
    batch, seq, d_model = x_prompt.shape
    db, dec_t, _ = x_sample.shape
    depth = w_in.shape[0]
    assert dec_t == 1, "sample group decodes one token per sequence"
    assert seq % BLOCK == 0 and N_GROUPS * 2 * (LANES // 2) == 2 * GMLP_WIDTH
    assert cache_k.shape[3:] == (N_HEADS, HEAD_DIM) and BLOCK % cache_k.shape[2] == 0
    past_len = page_table.shape[1] * cache_k.shape[2]
    alpha = (2 * depth) ** 0.25
    cos_p, sin_p = _rope_tables(jnp.arange(seq))
    cos_s, sin_s = _rope_tables(past_len + jnp.arange(dec_t))
    row = lambda a: a.reshape(1, -1)
    cache_kt = jnp.transpose(cache_k, (0, 1, 3, 4, 2))
    cache_vt = jnp.transpose(cache_v, (0, 1, 3, 4, 2))

    xp = x_prompt.reshape(batch * seq, d_model)
    xs = x_sample.reshape(db * dec_t, d_model)
    k_p, v_p, k_s, v_s, cv_s = [], [], [], [], []
    for l in range(depth):
        wf1i, wf1o = w_ffn1_in[l].astype(BF16), w_ffn1_out[l].astype(BF16)
        wf2i, wf2o = w_ffn2_in[l].astype(BF16), w_ffn2_out[l].astype(BF16)
        wi, wa, wb, wo = (w[l].astype(BF16) for w in (w_in, w_proj_a, w_proj_b, w_o))
        g1, b1, g2, b2, g3, b3 = (row(a[l]) for a in (ln1_g, ln1_b, ln2_g, ln2_b, ln3_g, ln3_b))
        lvg, lvb, bg = row(ln_v_g[l]), row(ln_v_b[l]), row(b_gate[l])
        gwid = GMLP_WIDTH // N_GROUPS
        bias_p = jnp.repeat(b_s[l].T, gwid, axis=1)
        scale_s = row(jnp.repeat(w_s[l, :, 0, 0], gwid))
        bias_s = row(jnp.repeat(b_s[l, :, 0], gwid))

        x1 = _ffn_ln(xp, wf1i, wf1o, g1, b1, alpha=alpha, tm=512)
        ya, q, k, v, hg = _mixer(x1, wi, lvg, lvb, cos_p, sin_p, w_s[l], bias_p, spatial=True, tm=256)
        yb = _moba_prompt(q, k, v, batch=batch, seq=seq)
        xp = _merge_ffn(x1, ya, yb, hg, bg, wa, wb, wo, g2, b2, wf2i, wf2o, g3, b3, alpha=alpha, tm=256)
        k_p.append(k.reshape(batch, seq, N_HEADS, HEAD_DIM))
        v_p.append(v.reshape(batch, seq, N_HEADS, HEAD_DIM))

        x1 = _ffn_ln(xs, wf1i, wf1o, g1, b1, alpha=alpha, tm=db)
        ya, q, k, v, hg, vg = _mixer(x1, wi, lvg, lvb, cos_s, sin_s, scale_s, bias_s, spatial=False, tm=db)
        q3, k3, v3 = (a.reshape(db, 1, ATT_WIDTH) for a in (q, k, v))
        logits, idx = _sample_logits(cache_kt, page_table, q3, l)
        yb = _moba_sample(page_table, idx.reshape(db, TOP_K * N_HEADS), q3, k3, v3, logits, cache_vt, l)
        yb = yb.reshape(db, ATT_WIDTH).astype(BF16)
        xs = _merge_ffn(x1, ya, yb, hg, bg, wa, wb, wo, g2, b2, wf2i, wf2o, g3, b3, alpha=alpha, tm=db)
        k_s.append(k.reshape(db, dec_t, N_HEADS, HEAD_DIM))
        v_s.append(v.reshape(db, dec_t, N_HEADS, HEAD_DIM))
        cv_s.append(vg.reshape(db, dec_t, GMLP_WIDTH))
    return (xp.reshape(batch, seq, d_model), xs.reshape(db, dec_t, d_model),
            jnp.stack(k_p), jnp.stack(v_p), jnp.stack(k_s), jnp.stack(v_s), jnp.stack(cv_s))
```

```python
import functools

import jax
import jax.numpy as jnp
from jax import lax
from jax.experimental import pallas as pl
from jax.experimental.pallas import tpu as pltpu

F32 = jnp.float32
BF16 = jnp.bfloat16

N_HEADS = 8
HEAD_DIM = 64
ATT_WIDTH = N_HEADS * HEAD_DIM
BLOCK = 256
TOP_K = 3
ROPE_THETA = 10000.0
CHUNK = 128
N_GROUPS = 8
GMLP_WIDTH = 512
LN_EPS = 1e-5
NEG_INF = -1e30
LANES = 128
FF_CHUNK = 256
VMEM_LIMIT = 56 * 1024 * 1024


def _resident(shape):
    zeros = (0,) * len(shape)
    return pl.BlockSpec(shape, lambda *_: zeros, pipeline_mode=pl.Buffered(1))


def _params(sem):
    return pltpu.CompilerParams(dimension_semantics=sem, vmem_limit_bytes=VMEM_LIMIT)


def _ln(y, g, b):
    mu = jnp.mean(y, axis=-1, keepdims=True)
    d = y - mu
    var = jnp.mean(d * d, axis=-1, keepdims=True)
    return d * lax.rsqrt(var + LN_EPS) * g + b


def _dot(a, b):
    return jnp.dot(a, b, preferred_element_type=F32)


def _dot_nt(a, b):
    return lax.dot_general(a, b, (((1,), (1,)), ((), ())), preferred_element_type=F32)


def _swiglu(xb, win_ref, wout_ref):
    d_ff = wout_ref.shape[0]
    acc = None
    for c in range(d_ff // FF_CHUNK):
        lo, hi = c * FF_CHUNK, (c + 1) * FF_CHUNK
        a = _dot(xb, win_ref[:, lo:hi])
        g = _dot(xb, win_ref[:, d_ff + lo:d_ff + hi])
        act = (g * jax.nn.sigmoid(g) * a).astype(BF16)
        part = _dot(act, wout_ref[lo:hi, :])
        acc = part if acc is None else acc + part
    return acc


def _ffn_ln_kernel(x_ref, win_ref, wout_ref, g_ref, b_ref, o_ref, *, alpha):
    x = x_ref[...]
    h = _swiglu(x.astype(BF16), win_ref, wout_ref)
    o_ref[...] = _ln(alpha * x + 0.5 * h, g_ref[...], b_ref[...])


def _ffn_ln(x, win, wout, g, b, *, alpha, tm):
    n, d = x.shape
    row = lambda i: (i, 0)
    return pl.pallas_call(
        functools.partial(_ffn_ln_kernel, alpha=alpha),
        grid=(n // tm,),
        in_specs=[pl.BlockSpec((tm, d), row), _resident(win.shape), _resident(wout.shape),
                  _resident(g.shape), _resident(b.shape)],
        out_specs=pl.BlockSpec((tm, d), row),
        out_shape=jax.ShapeDtypeStruct((n, d), F32),
        compiler_params=_params(("parallel",)),
        name="ffn_ln",
    )(x, win, wout, g, b)


def _rope(x, cos, sin_signed):
    lane = lax.broadcasted_iota(jnp.int32, x.shape, 1)
    first = (lane % HEAD_DIM) < (HEAD_DIM // 2)
    half = HEAD_DIM // 2
    partner = jnp.where(first, pltpu.roll(x, x.shape[1] - half, 1), pltpu.roll(x, half, 1))
    return x * cos + partner * sin_signed


def _mixer_kernel(x_ref, w_ref, lng_ref, lnb_ref, cos_ref, sin_ref, ws_ref, bs_ref, *out_refs, spatial):
    if spatial:
        ya_ref, q_ref, k_ref, v_ref, hg_ref, kb_ref, vb_ref, ksum_ref = out_refs
    else:
        ya_ref, q_ref, k_ref, v_ref, hg_ref, vg_ref = out_refs
    xb = x_ref[...].astype(BF16)
    gw, aw = GMLP_WIDTH, ATT_WIDTH

    def proj(lo, hi):
        return _dot(xb, w_ref[:, lo:hi])

    u = jax.nn.gelu(proj(0, gw))
    vg = _ln(jax.nn.gelu(proj(gw, 2 * gw)), lng_ref[...], lnb_ref[...])
    if spatial:
        tm = x_ref.shape[0]
        r = lax.broadcasted_iota(jnp.int32, (CHUNK, CHUNK), 0)
        c = lax.broadcasted_iota(jnp.int32, (CHUNK, CHUNK), 1)
        tril = r >= c
        lane = lax.broadcasted_iota(jnp.int32, (CHUNK, LANES), 1)
        lo_half = lane < (LANES // 2)
        for p in range(N_GROUPS // 2):
            wcat = jnp.concatenate(
                [jnp.where(tril, ws_ref[2 * p], 0.0), jnp.where(tril, ws_ref[2 * p + 1], 0.0)],
                axis=1).astype(BF16)
            cols = slice(p * LANES, (p + 1) * LANES)
            for ch in range(tm // CHUNK):
                rows = slice(ch * CHUNK, (ch + 1) * CHUNK)
                vp = vg[rows, cols]
                vstack = jnp.concatenate(
                    [jnp.where(lo_half, vp, 0.0), jnp.where(lo_half, 0.0, vp)], axis=0).astype(BF16)
                mixed = _dot(wcat, vstack) + bs_ref[:, cols]
                ya_ref[rows, cols] = (u[rows, cols] * mixed).astype(ya_ref.dtype)
    else:
        ya_ref[...] = (u * (vg * ws_ref[...] + bs_ref[...])).astype(ya_ref.dtype)
        vg_ref[...] = vg
    cos, sin = cos_ref[...], sin_ref[...]
    q_ref[...] = _rope(proj(2 * gw, 2 * gw + aw), cos, sin)
    k = _rope(proj(2 * gw + aw, 2 * gw + 2 * aw), cos, sin)
    v = proj(2 * gw + 2 * aw, 2 * gw + 3 * aw)
    if spatial:
        k_ref[0] = k.T
        v_ref[0] = v.T
        kb_ref[...] = k.astype(BF16)
        vb_ref[...] = v.astype(BF16)
        ksum_ref[0] = jnp.sum(k, axis=0, keepdims=True)
    else:
        k_ref[...] = k
        v_ref[...] = v
    hg_ref[...] = proj(2 * gw + 3 * aw, w_ref.shape[1])


def _mixer(x, w, lng, lnb, cos, sin, ws, bs, *, spatial, tm):
    n, d = x.shape
    gate_w = w.shape[1] - 2 * GMLP_WIDTH - 3 * ATT_WIDTH
    row = lambda i: (i, 0)
    if spatial:
        n_tab = cos.shape[0] // tm
        tab = pl.BlockSpec((tm, ATT_WIDTH), lambda i: (i % n_tab, 0))
    else:
        tab = _resident(cos.shape)
    outs = [(GMLP_WIDTH, BF16), (ATT_WIDTH, F32), (ATT_WIDTH, F32), (ATT_WIDTH, F32), (gate_w, F32)]
    outs += [(ATT_WIDTH, BF16), (ATT_WIDTH, BF16)] if spatial else [(GMLP_WIDTH, F32)]
    out_specs = [pl.BlockSpec((tm, wd), row) for wd, _ in outs]
    out_shape = [jax.ShapeDtypeStruct((n, wd), dt) for wd, dt in outs]
    if spatial:
        assert tm == BLOCK, "one mixer tile per MoBA key block (its key sum is the block sum)"
        out_specs.append(pl.BlockSpec((1, 1, ATT_WIDTH), lambda i: (i, 0, 0)))
        out_shape.append(jax.ShapeDtypeStruct((n // tm, 1, ATT_WIDTH), F32))
        for kv in (2, 3):
            out_specs[kv] = pl.BlockSpec((1, ATT_WIDTH, tm), lambda i: (i // n_tab, 0, i % n_tab))
            out_shape[kv] = jax.ShapeDtypeStruct((n // cos.shape[0], ATT_WIDTH, cos.shape[0]), F32)
    return pl.pallas_call(
        functools.partial(_mixer_kernel, spatial=spatial),
        grid=(n // tm,),
        in_specs=[pl.BlockSpec((tm, d), row), _resident(w.shape), _resident(lng.shape), _resident(lnb.shape),
                  tab, tab, _resident(ws.shape), _resident(bs.shape)],
        out_specs=out_specs,
        out_shape=out_shape,
        compiler_params=_params(("parallel",)),
        name="mixer_prompt" if spatial else "mixer_sample",
    )(x, w, lng, lnb, cos, sin, ws, bs)


def _moba_prompt_kernel(q_ref, k_ref, v_ref, ksum_ref, o_ref, kme_ref, sel_ref, lhs_ref, s_ref, mrun_ref, acc_ref,
                        *, nb):
    qb = pl.program_id(1)
    nh = N_HEADS
    width = q_ref.shape[1]

    @pl.when(qb == 0)
    def _():
        km = ksum_ref[...].reshape(nb, width) * (1.0 / BLOCK)
        head_of_lane = lax.broadcasted_iota(jnp.int32, (nb, width), 1) // HEAD_DIM
        kme_ref[...] = jnp.concatenate(
            [jnp.where(head_of_lane == h, km, 0.0) for h in range(nh) for _ in range(2)], axis=0)
        lane_b = lax.broadcasted_iota(jnp.int32, (BLOCK, LANES), 1)
        for j in range(nb):
            sel_ref[j] = jnp.where(lane_b == j, 1.0, 0.0).astype(BF16)

    q = q_ref[...]
    s = lax.dot_general(q, kme_ref[...], (((1,), (1,)), ((), ())),
                        precision=lax.Precision.HIGHEST, preferred_element_type=F32)
    lane = lax.broadcasted_iota(jnp.int32, s.shape, 1)
    jb = lane % nb
    s = jnp.where(jb < qb, s, NEG_INF)
    rank = jnp.zeros(s.shape, jnp.int32)
    for r in range(1, nb):
        sr = pltpu.roll(s, r, 1)
        jr = (jb + (nb - r)) % nb
        ahead = (sr > s) | ((sr == s) & (jr < jb))
        rank = rank + ahead.astype(jnp.int32)
    bias = jnp.where((rank < min(TOP_K, nb)) & (jb < qb), 0.0, NEG_INF)

    lo_half = lax.broadcasted_iota(jnp.int32, (BLOCK, LANES), 1) < HEAD_DIM
    rr = lax.broadcasted_iota(jnp.int32, (BLOCK, BLOCK), 0)
    cc = lax.broadcasted_iota(jnp.int32, (BLOCK, BLOCK), 1)
    causal = rr >= cc
    zeros_b = jnp.zeros((BLOCK, LANES), BF16)
    ones_b = jnp.ones((BLOCK, LANES), BF16)

    def rows_of(j):
        return pl.ds(pl.multiple_of(j * BLOCK, BLOCK), BLOCK)

    def half_max(x):
        return jnp.maximum(x[:, :LANES], x[:, LANES:])

    pair_cols = [slice(hp * LANES, (hp + 1) * LANES) for hp in range(nh // 2)]

    for h in range(nh):
        qp = q[:, pair_cols[h // 2]] * (HEAD_DIM ** -0.5)
        qh = jnp.where(lo_half, qp, 0.0) if h % 2 == 0 else jnp.where(lo_half, 0.0, qp)
        bh = pltpu.roll(bias, (LANES - (2 * nb * h + nb)) % LANES, 1)
        bh = jnp.where(lane < nb, bh, 0.0)
        lhs_ref[h] = jnp.concatenate([qh, bh], axis=1).astype(BF16)

    for h in range(nh):
        k_own = jnp.concatenate([k_ref[rows_of(qb), pair_cols[h // 2]], zeros_b], axis=1)
        sc = jnp.where(causal, _dot_nt(lhs_ref[h], k_own), NEG_INF)
        s_ref[h, qb] = sc
        mrun_ref[h] = half_max(sc)

    def logits_step(j, carry):
        sel_j = sel_ref[j]
        for h in range(nh):
            k_j = jnp.concatenate([k_ref[rows_of(j), pair_cols[h // 2]], sel_j], axis=1)
            sc = _dot_nt(lhs_ref[h], k_j)
            s_ref[h, j] = sc
            mrun_ref[h] = jnp.maximum(mrun_ref[h], half_max(sc))
        return carry

    lax.fori_loop(0, qb, logits_step, 0)

    for h in range(nh):
        m = jnp.max(mrun_ref[h], axis=1, keepdims=True)
        mrun_ref[h] = jnp.broadcast_to(m, (BLOCK, LANES))
        acc_ref[h] = jnp.zeros(acc_ref.shape[1:], F32)

    def pv_step(j, carry):
        for h in range(nh):
            v_j = jnp.concatenate([v_ref[rows_of(j), pair_cols[h // 2]], ones_b], axis=1)
            m = mrun_ref[h]
            p = jnp.exp(s_ref[h, j] - jnp.concatenate([m, m], axis=1)).astype(BF16)
            acc_ref[h] += _dot(p, v_j)
        return carry

    lax.fori_loop(0, qb + 1, pv_step, 0)
    for hp in range(nh // 2):
        out = [acc_ref[2 * hp + t, :, :LANES] / acc_ref[2 * hp + t, :, LANES:] for t in range(2)]
        o_ref[:, pair_cols[hp]] = jnp.where(lo_half, out[0], out[1]).astype(o_ref.dtype)


def _moba_prompt(q, kb, vb, ksum, *, batch, seq):
    nb = seq // BLOCK
    assert nb * BLOCK == seq and 2 * nb * N_HEADS == LANES, "prompt MoBA sweep expects 8 key blocks of 256"
    width = q.shape[1]
    return pl.pallas_call(
        functools.partial(_moba_prompt_kernel, nb=nb),
        grid=(batch, nb),
        in_specs=[pl.BlockSpec((BLOCK, width), lambda b, i: (b * nb + i, 0)),
                  pl.BlockSpec((seq, width), lambda b, i: (b, 0)),
                  pl.BlockSpec((seq, width), lambda b, i: (b, 0)),
                  pl.BlockSpec((nb, 1, width), lambda b, i: (b, 0, 0))],
        out_specs=pl.BlockSpec((BLOCK, width), lambda b, i: (b * nb + i, 0)),
        out_shape=jax.ShapeDtypeStruct(q.shape, BF16),
        scratch_shapes=[pltpu.VMEM((2 * nb * N_HEADS, width), F32),
                        pltpu.VMEM((nb, BLOCK, LANES), BF16),
                        pltpu.VMEM((N_HEADS, BLOCK, 2 * LANES), BF16),
                        pltpu.VMEM((N_HEADS, nb, BLOCK, BLOCK), F32),
                        pltpu.VMEM((N_HEADS, BLOCK, LANES), F32),
                        pltpu.VMEM((N_HEADS, BLOCK, 2 * LANES), F32)],
        compiler_params=_params(("parallel", "arbitrary")),
        name="moba_prompt",
    )(q, kb, vb, ksum)


def _column_broadcast(row):
    return jnp.transpose(jnp.broadcast_to(row, (LANES, row.shape[1])))


def _per_head_sum(x):
    return jnp.sum(x.reshape(N_HEADS, HEAD_DIM, x.shape[1]), axis=1)


def _logits_kernel(pt_ref, q_ref, ck_ref, lg_ref, idx_ref, kbuf, sem, *, layer, ppb):
    b = pl.program_id(0)
    slot = b % 2
    n_pages, rows = kbuf.shape[1], kbuf.shape[-1]

    def page_copy(seq, buf, i):
        return pltpu.make_async_copy(ck_ref.at[layer, pt_ref[seq, i]], kbuf.at[buf, i], sem.at[buf])

    @pl.when(b == 0)
    def _():
        for i in range(n_pages):
            page_copy(0, 0, i).start()

    @pl.when(b + 1 < pl.num_programs(0))
    def _():
        for i in range(n_pages):
            page_copy(b + 1, 1 - slot, i).start()

    qcol = _column_broadcast(q_ref[0] * (HEAD_DIM ** -0.5))
    for i in range(n_pages):
        page_copy(b, slot, i).wait()

    scores = []
    for j in range(n_pages // ppb):
        tot = None
        for p in range(ppb):
            i = j * ppb + p
            kt = kbuf[slot, i].reshape(N_HEADS * HEAD_DIM, rows)
            lg = _per_head_sum(kt * qcol)
            lg_ref[0, :, i * rows:(i + 1) * rows] = lg
            tot = lg if tot is None else tot + lg
        scores.append(jnp.sum(tot, axis=1, keepdims=True) * (HEAD_DIM ** 0.5 / BLOCK))

    nblk = len(scores)
    picks = []
    for _ in range(TOP_K):
        mx = functools.reduce(jnp.maximum, scores)
        ix = jnp.full(mx.shape, nblk, jnp.int32)
        for j in reversed(range(nblk)):
            ix = jnp.where(scores[j] == mx, j, ix)
        picks.append(ix)
        scores = [jnp.where(ix == j, -jnp.inf, sc) for j, sc in enumerate(scores)]
    idx_ref[0] = jnp.stack(picks, axis=0)


def _sample_logits(cache_kt, page_table, q, layer):
    _, _, nh, hd, rows = cache_kt.shape
    db, n_pages = page_table.shape
    ppb = BLOCK // rows
    assert n_pages // ppb >= TOP_K
    return pl.pallas_call(
        functools.partial(_logits_kernel, layer=layer, ppb=ppb),
        grid_spec=pltpu.PrefetchScalarGridSpec(
            num_scalar_prefetch=1,
            grid=(db,),
            in_specs=[pl.BlockSpec((1, 1, nh * hd), lambda b, pt: (b, 0, 0)), pl.BlockSpec(memory_space=pl.ANY)],
            out_specs=[pl.BlockSpec((1, nh, n_pages * rows), lambda b, pt: (b, 0, 0)),
                       pl.BlockSpec((1, TOP_K, nh, 1), lambda b, pt: (b, 0, 0, 0))],
            scratch_shapes=[pltpu.VMEM((2, n_pages, nh, hd, rows), F32), pltpu.SemaphoreType.DMA((2,))],
        ),
        out_shape=[jax.ShapeDtypeStruct((db, nh, n_pages * rows), F32),
                   jax.ShapeDtypeStruct((db, TOP_K, nh, 1), jnp.int32)],
        compiler_params=_params(("arbitrary",)),
        name="sample_logits",
    )(page_table, q, cache_kt)


def _moba_sample_kernel(pt_ref, idx_ref, q_ref, kn_ref, vn_ref, lg_ref, cv_ref, o_ref, vbuf, sem, *, layer, ppb):
    b = pl.program_id(0)
    slot = b % 2
    rows = cv_ref.shape[-1]
    slabs = [(h, r, pg) for h in range(N_HEADS) for r in range(TOP_K) for pg in range(ppb)]

    def slab_copy(seq, buf, h, r, pg):
        blk = idx_ref[seq, r * N_HEADS + h]
        phys = pt_ref[seq, blk * ppb + pg]
        return pltpu.make_async_copy(cv_ref.at[layer, phys, h], vbuf.at[buf, h, r * ppb + pg], sem.at[buf])

    @pl.when(b == 0)
    def _():
        for s in slabs:
            slab_copy(0, 0, *s).start()

    @pl.when(b + 1 < pl.num_programs(0))
    def _():
        for s in slabs:
            slab_copy(b + 1, 1 - slot, *s).start()

    s_sel = []
    for r in range(TOP_K):
        per_head = []
        for h in range(N_HEADS):
            start = pl.multiple_of(idx_ref[b, r * N_HEADS + h] * BLOCK, BLOCK)
            per_head.append(lg_ref[0, h:h + 1, pl.ds(start, BLOCK)])
        s_sel.append(jnp.concatenate(per_head, axis=0))
    qcol = _column_broadcast(q_ref[0] * (HEAD_DIM ** -0.5))
    s_self = _per_head_sum(qcol * _column_broadcast(kn_ref[0]))[:, :1]
    m = s_self
    for s in s_sel:
        m = jnp.maximum(m, jnp.max(s, axis=1, keepdims=True))
    p_sel = [jnp.exp(s - m) for s in s_sel]
    p_self = jnp.exp(s_self - m)
    denom = p_self
    for p in p_sel:
        denom = denom + jnp.sum(p, axis=1, keepdims=True)

    for s in slabs:
        slab_copy(b, slot, *s).wait()

    vn_col = _column_broadcast(vn_ref[0])[:, :1]
    out_cols = []
    for h in range(N_HEADS):
        acc = None
        for r in range(TOP_K):
            for pg in range(ppb):
                term = vbuf[slot, h, r * ppb + pg] * p_sel[r][h:h + 1, pg * rows:(pg + 1) * rows]
                acc = term if acc is None else acc + term
        num = jnp.sum(acc, axis=1, keepdims=True) + p_self[h:h + 1, :] * vn_col[h * HEAD_DIM:(h + 1) * HEAD_DIM, :]
        out_cols.append(num / denom[h:h + 1, :])
    out_col = jnp.concatenate(out_cols, axis=0)
    o_ref[0] = jnp.transpose(jnp.broadcast_to(out_col, (out_col.shape[0], LANES)))[:1, :]


def _moba_sample(page_table, idx, q, k_new, v_new, logits, cache_vt, layer):
    db, _, width = q.shape
    _, _, nh, hd, rows = cache_vt.shape
    ppb = BLOCK // rows
    seq_spec = pl.BlockSpec((1, 1, width), lambda b, pt, ix: (b, 0, 0))
    return pl.pallas_call(
        functools.partial(_moba_sample_kernel, layer=layer, ppb=ppb),
        grid_spec=pltpu.PrefetchScalarGridSpec(
            num_scalar_prefetch=2,
            grid=(db,),
            in_specs=[seq_spec, seq_spec, seq_spec,
                      pl.BlockSpec((1, nh, logits.shape[2]), lambda b, pt, ix: (b, 0, 0)),
                      pl.BlockSpec(memory_space=pl.ANY)],
            out_specs=seq_spec,
            scratch_shapes=[pltpu.VMEM((2, nh, TOP_K * ppb, hd, rows), F32), pltpu.SemaphoreType.DMA((2,))],
        ),
        out_shape=jax.ShapeDtypeStruct((db, 1, width), F32),
        compiler_params=_params(("arbitrary",)),
        name="moba_sample",
    )(page_table, idx, q, k_new, v_new, logits, cache_vt)


def _merge_ffn_kernel(x_ref, ya_ref, yb_ref, hg_ref, bg_ref, wa_ref, wb_ref, wo_ref, g2_ref, b2_ref,
                      win_ref, wout_ref, g3_ref, b3_ref, o_ref, *, alpha):
    x = x_ref[...]
    dm = x.shape[1]
    gates = jax.nn.sigmoid(hg_ref[...] + bg_ref[...])
    mix_in = gates[:, :dm] * _dot(ya_ref[...], wa_ref[...]) + gates[:, dm:] * _dot(yb_ref[...], wb_ref[...])
    x2 = _ln(alpha * x + _dot(mix_in.astype(BF16), wo_ref[...]), g2_ref[...], b2_ref[...])
    h = _swiglu(x2.astype(BF16), win_ref, wout_ref)
    o_ref[...] = _ln(alpha * x2 + 0.5 * h, g3_ref[...], b3_ref[...])


def _merge_ffn(x, ya, yb, hg, bg, wa, wb, wo, g2, b2, win, wout, g3, b3, *, alpha, tm):
    n, d = x.shape
    row = lambda i: (i, 0)
    tiles = [x, ya, yb, hg]
    consts = [bg, wa, wb, wo, g2, b2, win, wout, g3, b3]
    return pl.pallas_call(
        functools.partial(_merge_ffn_kernel, alpha=alpha),
        grid=(n // tm,),
        in_specs=[pl.BlockSpec((tm, a.shape[1]), row) for a in tiles] + [_resident(a.shape) for a in consts],
        out_specs=pl.BlockSpec((tm, d), row),
        out_shape=jax.ShapeDtypeStruct((n, d), F32),
        compiler_params=_params(("parallel",)),
        name="merge_ffn",
    )(*tiles, *consts)


def _rope_tables(pos):
    half = HEAD_DIM // 2
    inv = ROPE_THETA ** (-jnp.arange(half, dtype=F32) / half)
    ang = pos.astype(F32)[:, None] * inv[None, :]
    cos, sin = jnp.cos(ang), jnp.sin(ang)
    return (jnp.tile(jnp.concatenate([cos, cos], -1), (1, N_HEADS)),
            jnp.tile(jnp.concatenate([-sin, sin], -1), (1, N_HEADS)))


def kernel(x_prompt, x_sample, cache_k, cache_v, page_table, w_ffn1_in, w_ffn1_out, ln1_g, ln1_b, w_in, w_s, b_s,
           ln_v_g, ln_v_b, b_gate, w_proj_a, w_proj_b, w_o, ln2_g, ln2_b, w_ffn2_in, w_ffn2_out, ln3_g, ln3_b):
    batch, seq, d_model = x_prompt.shape
    db, dec_t, _ = x_sample.shape
    depth = w_in.shape[0]
    assert dec_t == 1, "sample group decodes one token per sequence"
    assert seq % BLOCK == 0 and N_GROUPS * 2 * (LANES // 2) == 2 * GMLP_WIDTH
    assert cache_k.shape[3:] == (N_HEADS, HEAD_DIM) and BLOCK % cache_k.shape[2] == 0
    past_len = page_table.shape[1] * cache_k.shape[2]
    alpha = (2 * depth) ** 0.25
    cos_p, sin_p = _rope_tables(jnp.arange(seq))
    cos_s, sin_s = _rope_tables(past_len + jnp.arange(dec_t))
    row = lambda a: a.reshape(1, -1)
    cache_kt = jnp.transpose(cache_k, (0, 1, 3, 4, 2))
    cache_vt = jnp.transpose(cache_v, (0, 1, 3, 4, 2))

    xp = x_prompt.reshape(batch * seq, d_model)
    xs = x_sample.reshape(db * dec_t, d_model)
    k_p, v_p, k_s, v_s, cv_s = [], [], [], [], []
    for l in range(depth):
        wf1i, wf1o = w_ffn1_in[l].astype(BF16), w_ffn1_out[l].astype(BF16)
        wf2i, wf2o = w_ffn2_in[l].astype(BF16), w_ffn2_out[l].astype(BF16)
        wi, wa, wb, wo = (w[l].astype(BF16) for w in (w_in, w_proj_a, w_proj_b, w_o))
        g1, b1, g2, b2, g3, b3 = (row(a[l]) for a in (ln1_g, ln1_b, ln2_g, ln2_b, ln3_g, ln3_b))
        lvg, lvb, bg = row(ln_v_g[l]), row(ln_v_b[l]), row(b_gate[l])
        gwid = GMLP_WIDTH // N_GROUPS
        bias_p = jnp.repeat(b_s[l].T, gwid, axis=1)
        scale_s = row(jnp.repeat(w_s[l, :, 0, 0], gwid))
        bias_s = row(jnp.repeat(b_s[l, :, 0], gwid))

        x1 = _ffn_ln(xp, wf1i, wf1o, g1, b1, alpha=alpha, tm=512)
        ya, q, kt, vt, hg, kb, vb, ksum = _mixer(x1, wi, lvg, lvb, cos_p, sin_p, w_s[l], bias_p,
                                                 spatial=True, tm=BLOCK)
        yb = _moba_prompt(q, kb, vb, ksum, batch=batch, seq=seq)
        xp = _merge_ffn(x1, ya, yb, hg, bg, wa, wb, wo, g2, b2, wf2i, wf2o, g3, b3, alpha=alpha, tm=512)
        k_p.append(kt.reshape(batch, N_HEADS, HEAD_DIM, seq).transpose(0, 3, 1, 2))
        v_p.append(vt.reshape(batch, N_HEADS, HEAD_DIM, seq).transpose(0, 3, 1, 2))

        x1 = _ffn_ln(xs, wf1i, wf1o, g1, b1, alpha=alpha, tm=db)
        ya, q, k, v, hg, vg = _mixer(x1, wi, lvg, lvb, cos_s, sin_s, scale_s, bias_s, spatial=False, tm=db)
        q3, k3, v3 = (a.reshape(db, 1, ATT_WIDTH) for a in (q, k, v))
        logits, idx = _sample_logits(cache_kt, page_table, q3, l)
        yb = _moba_sample(page_table, idx.reshape(db, TOP_K * N_HEADS), q3, k3, v3, logits, cache_vt, l)
        yb = yb.reshape(db, ATT_WIDTH).astype(BF16)
        xs = _merge_ffn(x1, ya, yb, hg, bg, wa, wb, wo, g2, b2, wf2i, wf2o, g3, b3, alpha=alpha, tm=db)
        k_s.append(k.reshape(db, dec_t, N_HEADS, HEAD_DIM))
        v_s.append(v.reshape(db, dec_t, N_HEADS, HEAD_DIM))
        cv_s.append(vg.reshape(db, dec_t, GMLP_WIDTH))
    return (xp.reshape(batch, seq, d_model), xs.reshape(db, dec_t, d_model),
            jnp.stack(k_p), jnp.stack(v_p), jnp.stack(k_s), jnp.stack(v_s), jnp.stack(cv_s))
```

```python
import functools

import jax
import jax.numpy as jnp
from jax import lax
from jax.experimental import pallas as pl
from jax.experimental.pallas import tpu as pltpu

F32 = jnp.float32
BF16 = jnp.bfloat16

N_HEADS = 8
HEAD_DIM = 64
ATT_WIDTH = N_HEADS * HEAD_DIM
BLOCK = 256
TOP_K = 3
ROPE_THETA = 10000.0
CHUNK = 128
N_GROUPS = 8
GMLP_WIDTH = 512
LN_EPS = 1e-5
NEG_INF = -1e30
LANES = 128
FF_CHUNK = 256
VMEM_LIMIT = 56 * 1024 * 1024


def _resident(shape):
    zeros = (0,) * len(shape)
    return pl.BlockSpec(shape, lambda *_: zeros, pipeline_mode=pl.Buffered(1))


def _params(sem):
    return pltpu.CompilerParams(dimension_semantics=sem, vmem_limit_bytes=VMEM_LIMIT)


def _ln(y, g, b):
    mu = jnp.mean(y, axis=-1, keepdims=True)
    d = y - mu
    var = jnp.mean(d * d, axis=-1, keepdims=True)
    return d * lax.rsqrt(var + LN_EPS) * g + b


def _dot(a, b):
    return jnp.dot(a, b, preferred_element_type=F32)


def _dot_nt(a, b):
    return lax.dot_general(a, b, (((1,), (1,)), ((), ())), preferred_element_type=F32)


def _swiglu(xb, win_ref, wout_ref, chunks=None, acc=None):
    d_ff = wout_ref.shape[0]
    for c in (range(d_ff // FF_CHUNK) if chunks is None else chunks):
        lo, hi = c * FF_CHUNK, (c + 1) * FF_CHUNK
        a = _dot(xb, win_ref[:, lo:hi])
        g = _dot(xb, win_ref[:, d_ff + lo:d_ff + hi])
        act = (g * jax.nn.sigmoid(g) * a).astype(BF16)
        part = _dot(act, wout_ref[lo:hi, :])
        acc = part if acc is None else acc + part
    return acc


def _ffn_ln_kernel(x_ref, win_ref, wout_ref, g_ref, b_ref, o_ref, *, alpha):
    x = x_ref[...]
    h = _swiglu(x.astype(BF16), win_ref, wout_ref)
    o_ref[...] = _ln(alpha * x + 0.5 * h, g_ref[...], b_ref[...])


def _ffn_ln(x, win, wout, g, b, *, alpha, tm):
    n, d = x.shape
    row = lambda i: (i, 0)
    return pl.pallas_call(
        functools.partial(_ffn_ln_kernel, alpha=alpha),
        grid=(n // tm,),
        in_specs=[pl.BlockSpec((tm, d), row), _resident(win.shape), _resident(wout.shape),
                  _resident(g.shape), _resident(b.shape)],
        out_specs=pl.BlockSpec((tm, d), row),
        out_shape=jax.ShapeDtypeStruct((n, d), F32),
        compiler_params=_params(("parallel",)),
        name="ffn_ln",
    )(x, win, wout, g, b)


def _rope(x, cos, sin_signed):
    lane = lax.broadcasted_iota(jnp.int32, x.shape, 1)
    first = (lane % HEAD_DIM) < (HEAD_DIM // 2)
    half = HEAD_DIM // 2
    partner = jnp.where(first, pltpu.roll(x, x.shape[1] - half, 1), pltpu.roll(x, half, 1))
    return x * cos + partner * sin_signed


def _mixer_kernel(x_ref, w_ref, lng_ref, lnb_ref, cos_ref, sin_ref, ws_ref, bs_ref, *out_refs, spatial):
    if spatial:
        ya_ref, q_ref, k_ref, v_ref, hg_ref, kb_ref, vb_ref, ksum_ref = out_refs
    else:
        ya_ref, q_ref, k_ref, v_ref, hg_ref, vg_ref = out_refs
    xb = x_ref[...].astype(BF16)
    gw, aw = GMLP_WIDTH, ATT_WIDTH

    def proj(lo, hi):
        return _dot(xb, w_ref[:, lo:hi])

    u = jax.nn.gelu(proj(0, gw))
    vg = _ln(jax.nn.gelu(proj(gw, 2 * gw)), lng_ref[...], lnb_ref[...])
    if spatial:
        tm = x_ref.shape[0]
        r = lax.broadcasted_iota(jnp.int32, (CHUNK, CHUNK), 0)
        c = lax.broadcasted_iota(jnp.int32, (CHUNK, CHUNK), 1)
        tril = r >= c
        lane = lax.broadcasted_iota(jnp.int32, (CHUNK, LANES), 1)
        lo_half = lane < (LANES // 2)
        for p in range(N_GROUPS // 2):
            wcat = jnp.concatenate(
                [jnp.where(tril, ws_ref[2 * p], 0.0), jnp.where(tril, ws_ref[2 * p + 1], 0.0)],
                axis=1).astype(BF16)
            cols = slice(p * LANES, (p + 1) * LANES)
            for ch in range(tm // CHUNK):
                rows = slice(ch * CHUNK, (ch + 1) * CHUNK)
                vp = vg[rows, cols]
                vstack = jnp.concatenate(
                    [jnp.where(lo_half, vp, 0.0), jnp.where(lo_half, 0.0, vp)], axis=0).astype(BF16)
                mixed = _dot(wcat, vstack) + bs_ref[:, cols]
                ya_ref[rows, cols] = (u[rows, cols] * mixed).astype(ya_ref.dtype)
    else:
        ya_ref[...] = (u * (vg * ws_ref[...] + bs_ref[...])).astype(ya_ref.dtype)
        vg_ref[...] = vg
    cos, sin = cos_ref[...], sin_ref[...]
    q_ref[...] = _rope(proj(2 * gw, 2 * gw + aw), cos, sin)
    k = _rope(proj(2 * gw + aw, 2 * gw + 2 * aw), cos, sin)
    v = proj(2 * gw + 2 * aw, 2 * gw + 3 * aw)
    if spatial:
        k_ref[0] = k.T
        v_ref[0] = v.T
        kb_ref[...] = k.astype(BF16)
        vb_ref[...] = v.astype(BF16)
        ksum_ref[0] = jnp.sum(k, axis=0, keepdims=True)
    else:
        k_ref[...] = k
        v_ref[...] = v
    hg_ref[...] = proj(2 * gw + 3 * aw, w_ref.shape[1])


def _mixer(x, w, lng, lnb, cos, sin, ws, bs, *, spatial, tm):
    n, d = x.shape
    gate_w = w.shape[1] - 2 * GMLP_WIDTH - 3 * ATT_WIDTH
    row = lambda i: (i, 0)
    if spatial:
        n_tab = cos.shape[0] // tm
        tab = pl.BlockSpec((tm, ATT_WIDTH), lambda i: (i % n_tab, 0))
    else:
        tab = _resident(cos.shape)
    outs = [(GMLP_WIDTH, BF16), (ATT_WIDTH, F32), (ATT_WIDTH, F32), (ATT_WIDTH, F32), (gate_w, F32)]
    outs += [(ATT_WIDTH, BF16), (ATT_WIDTH, BF16)] if spatial else [(GMLP_WIDTH, F32)]
    out_specs = [pl.BlockSpec((tm, wd), row) for wd, _ in outs]
    out_shape = [jax.ShapeDtypeStruct((n, wd), dt) for wd, dt in outs]
    if spatial:
        assert tm == BLOCK, "one mixer tile per MoBA key block (its key sum is the block sum)"
        out_specs.append(pl.BlockSpec((1, 1, ATT_WIDTH), lambda i: (i, 0, 0)))
        out_shape.append(jax.ShapeDtypeStruct((n // tm, 1, ATT_WIDTH), F32))
        for kv in (2, 3):
            out_specs[kv] = pl.BlockSpec((1, ATT_WIDTH, tm), lambda i: (i // n_tab, 0, i % n_tab))
            out_shape[kv] = jax.ShapeDtypeStruct((n // cos.shape[0], ATT_WIDTH, cos.shape[0]), F32)
    return pl.pallas_call(
        functools.partial(_mixer_kernel, spatial=spatial),
        grid=(n // tm,),
        in_specs=[pl.BlockSpec((tm, d), row), _resident(w.shape), _resident(lng.shape), _resident(lnb.shape),
                  tab, tab, _resident(ws.shape), _resident(bs.shape)],
        out_specs=out_specs,
        out_shape=out_shape,
        compiler_params=_params(("parallel",)),
        name="mixer_prompt" if spatial else "mixer_sample",
    )(x, w, lng, lnb, cos, sin, ws, bs)


def _moba_prompt_kernel(q_ref, k_ref, v_ref, ksum_ref, o_ref, kme_ref, sel_ref, lhs_ref, s_ref, mrun_ref, acc_ref,
                        *, nb):
    qb = pl.program_id(1)
    nh = N_HEADS
    width = q_ref.shape[1]

    @pl.when(qb == 0)
    def _():
        km = ksum_ref[...].reshape(nb, width) * (1.0 / BLOCK)
        head_of_lane = lax.broadcasted_iota(jnp.int32, (nb, width), 1) // HEAD_DIM
        kme_ref[...] = jnp.concatenate(
            [jnp.where(head_of_lane == h, km, 0.0) for h in range(nh) for _ in range(2)], axis=0)
        lane_b = lax.broadcasted_iota(jnp.int32, (BLOCK, LANES), 1)
        for j in range(nb):
            sel_ref[j] = jnp.where(lane_b == j, 1.0, 0.0).astype(BF16)

    q = q_ref[...]
    s = lax.dot_general(q, kme_ref[...], (((1,), (1,)), ((), ())),
                        precision=lax.Precision.HIGHEST, preferred_element_type=F32)
    lane = lax.broadcasted_iota(jnp.int32, s.shape, 1)
    jb = lane % nb
    s = jnp.where(jb < qb, s, NEG_INF)
    rank = jnp.zeros(s.shape, jnp.int32)
    for r in range(1, nb):
        sr = pltpu.roll(s, r, 1)
        jr = (jb + (nb - r)) % nb
        ahead = (sr > s) | ((sr == s) & (jr < jb))
        rank = rank + ahead.astype(jnp.int32)
    bias = jnp.where((rank < min(TOP_K, nb)) & (jb < qb), 0.0, NEG_INF)

    lo_half = lax.broadcasted_iota(jnp.int32, (BLOCK, LANES), 1) < HEAD_DIM
    rr = lax.broadcasted_iota(jnp.int32, (BLOCK, BLOCK), 0)
    cc = lax.broadcasted_iota(jnp.int32, (BLOCK, BLOCK), 1)
    causal = rr >= cc
    zeros_b = jnp.zeros((BLOCK, LANES), BF16)
    ones_b = jnp.ones((BLOCK, LANES), BF16)

    def rows_of(j):
        return pl.ds(pl.multiple_of(j * BLOCK, BLOCK), BLOCK)

    def half_max(x):
        return jnp.maximum(x[:, :LANES], x[:, LANES:])

    pair_cols = [slice(hp * LANES, (hp + 1) * LANES) for hp in range(nh // 2)]

    for h in range(nh):
        qp = q[:, pair_cols[h // 2]] * (HEAD_DIM ** -0.5)
        qh = jnp.where(lo_half, qp, 0.0) if h % 2 == 0 else jnp.where(lo_half, 0.0, qp)
        bh = pltpu.roll(bias, (LANES - (2 * nb * h + nb)) % LANES, 1)
        bh = jnp.where(lane < nb, bh, 0.0)
        lhs_ref[h] = jnp.concatenate([qh, bh], axis=1).astype(BF16)

    for h in range(nh):
        k_own = jnp.concatenate([k_ref[rows_of(qb), pair_cols[h // 2]], zeros_b], axis=1)
        sc = jnp.where(causal, _dot_nt(lhs_ref[h], k_own), NEG_INF)
        s_ref[h, qb] = sc
        mrun_ref[h] = half_max(sc)

    def logits_step(j, carry):
        sel_j = sel_ref[j]
        for h in range(nh):
            k_j = jnp.concatenate([k_ref[rows_of(j), pair_cols[h // 2]], sel_j], axis=1)
            sc = _dot_nt(lhs_ref[h], k_j)
            s_ref[h, j] = sc
            mrun_ref[h] = jnp.maximum(mrun_ref[h], half_max(sc))
        return carry

    lax.fori_loop(0, qb, logits_step, 0)

    for h in range(nh):
        m = jnp.max(mrun_ref[h], axis=1, keepdims=True)
        mrun_ref[h] = jnp.broadcast_to(m, (BLOCK, LANES))
        acc_ref[h] = jnp.zeros(acc_ref.shape[1:], F32)

    def pv_step(j, carry):
        for h in range(nh):
            v_j = jnp.concatenate([v_ref[rows_of(j), pair_cols[h // 2]], ones_b], axis=1)
            m = mrun_ref[h]
            p = jnp.exp(s_ref[h, j] - jnp.concatenate([m, m], axis=1)).astype(BF16)
            acc_ref[h] += _dot(p, v_j)
        return carry

    lax.fori_loop(0, qb + 1, pv_step, 0)
    for hp in range(nh // 2):
        out = [acc_ref[2 * hp + t, :, :LANES] / acc_ref[2 * hp + t, :, LANES:] for t in range(2)]
        o_ref[:, pair_cols[hp]] = jnp.where(lo_half, out[0], out[1]).astype(o_ref.dtype)


def _moba_prompt(q, kb, vb, ksum, *, batch, seq):
    nb = seq // BLOCK
    assert nb * BLOCK == seq and 2 * nb * N_HEADS == LANES, "prompt MoBA sweep expects 8 key blocks of 256"
    width = q.shape[1]
    return pl.pallas_call(
        functools.partial(_moba_prompt_kernel, nb=nb),
        grid=(batch, nb),
        in_specs=[pl.BlockSpec((BLOCK, width), lambda b, i: (b * nb + i, 0)),
                  pl.BlockSpec((seq, width), lambda b, i: (b, 0)),
                  pl.BlockSpec((seq, width), lambda b, i: (b, 0)),
                  pl.BlockSpec((nb, 1, width), lambda b, i: (b, 0, 0))],
        out_specs=pl.BlockSpec((BLOCK, width), lambda b, i: (b * nb + i, 0)),
        out_shape=jax.ShapeDtypeStruct(q.shape, BF16),
        scratch_shapes=[pltpu.VMEM((2 * nb * N_HEADS, width), F32),
                        pltpu.VMEM((nb, BLOCK, LANES), BF16),
                        pltpu.VMEM((N_HEADS, BLOCK, 2 * LANES), BF16),
                        pltpu.VMEM((N_HEADS, nb, BLOCK, BLOCK), F32),
                        pltpu.VMEM((N_HEADS, BLOCK, LANES), F32),
                        pltpu.VMEM((N_HEADS, BLOCK, 2 * LANES), F32)],
        compiler_params=_params(("parallel", "arbitrary")),
        name="moba_prompt",
    )(q, kb, vb, ksum)


def _column_broadcast(row):
    return jnp.transpose(jnp.broadcast_to(row, (LANES, row.shape[1])))


def _per_head_sum(x):
    return jnp.sum(x.reshape(N_HEADS, HEAD_DIM, x.shape[1]), axis=1)


def _sweep_ffn_kernel(pt_ref, q_ref, ck_ref, x_ref, win_ref, wout_ref, g_ref, b_ref,
                      lg_ref, idx_ref, o_ref, kbuf, sem, part_ref, *, layer, ppb, alpha):
    b = pl.program_id(0)
    half, rows = kbuf.shape[1], kbuf.shape[-1]
    n_chunks = wout_ref.shape[0] // FF_CHUNK
    ffn_part = (range(0, n_chunks // 2), range(n_chunks // 2, n_chunks))
    sub = HEAD_DIM // 8

    def page_copy(seq, part, i):
        return pltpu.make_async_copy(ck_ref.at[layer, pt_ref[seq, part * half + i]], kbuf.at[part, i], sem.at[part])

    @pl.when(b == 0)
    def _():
        for part in range(2):
            for i in range(half):
                page_copy(0, part, i).start()

    qcol = _column_broadcast(q_ref[0] * (HEAD_DIM ** -0.5))
    x = x_ref[...]
    xb = x.astype(BF16)
    ffn = None
    scores = []
    for part in range(2):
        for i in range(half):
            page_copy(b, part, i).wait()
        for h in range(N_HEADS):
            qh = qcol[h * HEAD_DIM:(h + 1) * HEAD_DIM]
            for i in range(half):
                part_ref[i, h] = jnp.sum((kbuf[part, i, h] * qh).reshape(sub, 8, rows), axis=0)
        for j in range(half // ppb):
            tot = None
            for p in range(ppb):
                i = j * ppb + p
                lg = jnp.sum(part_ref[i], axis=1)
                lg_ref[0, :, (part * half + i) * rows:(part * half + i + 1) * rows] = lg
                tot = lg if tot is None else tot + lg
            scores.append(jnp.sum(tot, axis=1, keepdims=True) * (HEAD_DIM ** 0.5 / BLOCK))
        ffn = _swiglu(xb, win_ref, wout_ref, ffn_part[part], ffn)

        @pl.when(b + 1 < pl.num_programs(0))
        def _():
            for i in range(half):
                page_copy(b + 1, part, i).start()

    o_ref[...] = _ln(alpha * x + 0.5 * ffn, g_ref[...], b_ref[...])
    nblk = len(scores)
    picks = []
    for _ in range(TOP_K):
        mx = functools.reduce(jnp.maximum, scores)
        ix = jnp.full(mx.shape, nblk, jnp.int32)
        for j in reversed(range(nblk)):
            ix = jnp.where(scores[j] == mx, j, ix)
        picks.append(ix)
        scores = [jnp.where(ix == j, -jnp.inf, sc) for j, sc in enumerate(scores)]
    idx_ref[0] = jnp.stack(picks, axis=0)


def _sweep_ffn(cache_kt, page_table, q, layer, x, win, wout, g, b, *, alpha):
    _, _, nh, hd, rows = cache_kt.shape
    db, n_pages = page_table.shape
    n, d = x.shape
    ppb = BLOCK // rows
    tm = n // db
    assert n_pages // ppb >= TOP_K and n_pages % (2 * ppb) == 0 and hd % 8 == 0
    assert tm * db == n and tm % 8 == 0, "one prompt row tile per sample sequence"
    const = lambda shape: pl.BlockSpec(shape, lambda i, pt: (0,) * len(shape), pipeline_mode=pl.Buffered(1))
    return pl.pallas_call(
        functools.partial(_sweep_ffn_kernel, layer=layer, ppb=ppb, alpha=alpha),
        grid_spec=pltpu.PrefetchScalarGridSpec(
            num_scalar_prefetch=1,
            grid=(db,),
            in_specs=[pl.BlockSpec((1, 1, nh * hd), lambda i, pt: (i, 0, 0)), pl.BlockSpec(memory_space=pl.ANY),
                      pl.BlockSpec((tm, d), lambda i, pt: (i, 0)),
                      const(win.shape), const(wout.shape), const(g.shape), const(b.shape)],
            out_specs=[pl.BlockSpec((1, nh, n_pages * rows), lambda i, pt: (i, 0, 0)),
                       pl.BlockSpec((1, TOP_K, nh, 1), lambda i, pt: (i, 0, 0, 0)),
                       pl.BlockSpec((tm, d), lambda i, pt: (i, 0))],
            scratch_shapes=[pltpu.VMEM((2, n_pages // 2, nh, hd, rows), F32), pltpu.SemaphoreType.DMA((2,)),
                            pltpu.VMEM((n_pages // 2, nh, 8, rows), F32)],
        ),
        out_shape=[jax.ShapeDtypeStruct((db, nh, n_pages * rows), F32),
                   jax.ShapeDtypeStruct((db, TOP_K, nh, 1), jnp.int32),
                   jax.ShapeDtypeStruct((n, d), F32)],
        compiler_params=_params(("arbitrary",)),
        name="sweep_ffn",
    )(page_table, q, cache_kt, x, win, wout, g, b)


def _moba_sample_kernel(pt_ref, idx_ref, q_ref, kn_ref, vn_ref, lg_ref, cv_ref, o_ref, vbuf, sem, *, layer, ppb):
    b = pl.program_id(0)
    slot = b % 2
    rows = cv_ref.shape[-1]
    slabs = [(h, r, pg) for h in range(N_HEADS) for r in range(TOP_K) for pg in range(ppb)]

    def slab_copy(seq, buf, h, r, pg):
        blk = idx_ref[seq, r * N_HEADS + h]
        phys = pt_ref[seq, blk * ppb + pg]
        return pltpu.make_async_copy(cv_ref.at[layer, phys, h], vbuf.at[buf, h, r * ppb + pg], sem.at[buf])

    @pl.when(b == 0)
    def _():
        for s in slabs:
            slab_copy(0, 0, *s).start()

    @pl.when(b + 1 < pl.num_programs(0))
    def _():
        for s in slabs:
            slab_copy(b + 1, 1 - slot, *s).start()

    s_sel = []
    for r in range(TOP_K):
        per_head = []
        for h in range(N_HEADS):
            start = pl.multiple_of(idx_ref[b, r * N_HEADS + h] * BLOCK, BLOCK)
            per_head.append(lg_ref[0, h:h + 1, pl.ds(start, BLOCK)])
        s_sel.append(jnp.concatenate(per_head, axis=0))
    qcol = _column_broadcast(q_ref[0] * (HEAD_DIM ** -0.5))
    s_self = _per_head_sum(qcol * _column_broadcast(kn_ref[0]))[:, :1]
    m = s_self
    for s in s_sel:
        m = jnp.maximum(m, jnp.max(s, axis=1, keepdims=True))
    p_sel = [jnp.exp(s - m) for s in s_sel]
    p_self = jnp.exp(s_self - m)
    denom = p_self
    for p in p_sel:
        denom = denom + jnp.sum(p, axis=1, keepdims=True)

    for s in slabs:
        slab_copy(b, slot, *s).wait()

    vn_col = _column_broadcast(vn_ref[0])[:, :1]
    out_cols = []
    for h in range(N_HEADS):
        acc = None
        for r in range(TOP_K):
            for pg in range(ppb):
                term = vbuf[slot, h, r * ppb + pg] * p_sel[r][h:h + 1, pg * rows:(pg + 1) * rows]
                acc = term if acc is None else acc + term
        num = jnp.sum(acc, axis=1, keepdims=True) + p_self[h:h + 1, :] * vn_col[h * HEAD_DIM:(h + 1) * HEAD_DIM, :]
        out_cols.append(num / denom[h:h + 1, :])
    out_col = jnp.concatenate(out_cols, axis=0)
    o_ref[0] = jnp.transpose(jnp.broadcast_to(out_col, (out_col.shape[0], LANES)))[:1, :]


def _moba_sample(page_table, idx, q, k_new, v_new, logits, cache_vt, layer):
    db, _, width = q.shape
    _, _, nh, hd, rows = cache_vt.shape
    ppb = BLOCK // rows
    seq_spec = pl.BlockSpec((1, 1, width), lambda b, pt, ix: (b, 0, 0))
    return pl.pallas_call(
        functools.partial(_moba_sample_kernel, layer=layer, ppb=ppb),
        grid_spec=pltpu.PrefetchScalarGridSpec(
            num_scalar_prefetch=2,
            grid=(db,),
            in_specs=[seq_spec, seq_spec, seq_spec,
                      pl.BlockSpec((1, nh, logits.shape[2]), lambda b, pt, ix: (b, 0, 0)),
                      pl.BlockSpec(memory_space=pl.ANY)],
            out_specs=seq_spec,
            scratch_shapes=[pltpu.VMEM((2, nh, TOP_K * ppb, hd, rows), F32), pltpu.SemaphoreType.DMA((2,))],
        ),
        out_shape=jax.ShapeDtypeStruct((db, 1, width), F32),
        compiler_params=_params(("arbitrary",)),
        name="moba_sample",
    )(page_table, idx, q, k_new, v_new, logits, cache_vt)


def _merge_ffn_kernel(x_ref, ya_ref, yb_ref, hg_ref, bg_ref, wa_ref, wb_ref, wo_ref, g2_ref, b2_ref,
                      win_ref, wout_ref, g3_ref, b3_ref, o_ref, *, alpha):
    x = x_ref[...]
    dm = x.shape[1]
    gates = jax.nn.sigmoid(hg_ref[...] + bg_ref[...])
    mix_in = gates[:, :dm] * _dot(ya_ref[...], wa_ref[...]) + gates[:, dm:] * _dot(yb_ref[...], wb_ref[...])
    x2 = _ln(alpha * x + _dot(mix_in.astype(BF16), wo_ref[...]), g2_ref[...], b2_ref[...])
    h = _swiglu(x2.astype(BF16), win_ref, wout_ref)
    o_ref[...] = _ln(alpha * x2 + 0.5 * h, g3_ref[...], b3_ref[...])


def _merge_ffn(x, ya, yb, hg, bg, wa, wb, wo, g2, b2, win, wout, g3, b3, *, alpha, tm):
    n, d = x.shape
    row = lambda i: (i, 0)
    tiles = [x, ya, yb, hg]
    consts = [bg, wa, wb, wo, g2, b2, win, wout, g3, b3]
    return pl.pallas_call(
        functools.partial(_merge_ffn_kernel, alpha=alpha),
        grid=(n // tm,),
        in_specs=[pl.BlockSpec((tm, a.shape[1]), row) for a in tiles] + [_resident(a.shape) for a in consts],
        out_specs=pl.BlockSpec((tm, d), row),
        out_shape=jax.ShapeDtypeStruct((n, d), F32),
        compiler_params=_params(("parallel",)),
        name="merge_ffn",
    )(*tiles, *consts)


def _rope_tables(pos):
    half = HEAD_DIM // 2
    inv = ROPE_THETA ** (-jnp.arange(half, dtype=F32) / half)
    ang = pos.astype(F32)[:, None] * inv[None, :]
    cos, sin = jnp.cos(ang), jnp.sin(ang)
    return (jnp.tile(jnp.concatenate([cos, cos], -1), (1, N_HEADS)),
            jnp.tile(jnp.concatenate([-sin, sin], -1), (1, N_HEADS)))


def kernel(x_prompt, x_sample, cache_k, cache_v, page_table, w_ffn1_in, w_ffn1_out, ln1_g, ln1_b, w_in, w_s, b_s,
           ln_v_g, ln_v_b, b_gate, w_proj_a, w_proj_b, w_o, ln2_g, ln2_b, w_ffn2_in, w_ffn2_out, ln3_g, ln3_b):
    batch, seq, d_model = x_prompt.shape
    db, dec_t, _ = x_sample.shape
    depth = w_in.shape[0]
    assert dec_t == 1, "sample group decodes one token per sequence"
    assert seq % BLOCK == 0 and N_GROUPS * 2 * (LANES // 2) == 2 * GMLP_WIDTH
    assert cache_k.shape[3:] == (N_HEADS, HEAD_DIM) and BLOCK % cache_k.shape[2] == 0
    past_len = page_table.shape[1] * cache_k.shape[2]
    alpha = (2 * depth) ** 0.25
    cos_p, sin_p = _rope_tables(jnp.arange(seq))
    cos_s, sin_s = _rope_tables(past_len + jnp.arange(dec_t))
    row = lambda a: a.reshape(1, -1)
    cache_kt = jnp.transpose(cache_k, (0, 1, 3, 4, 2))
    cache_vt = jnp.transpose(cache_v, (0, 1, 3, 4, 2))

    xp = x_prompt.reshape(batch * seq, d_model)
    xs = x_sample.reshape(db * dec_t, d_model)
    k_p, v_p, k_s, v_s, cv_s = [], [], [], [], []
    for l in range(depth):
        wf1i, wf1o = w_ffn1_in[l].astype(BF16), w_ffn1_out[l].astype(BF16)
        wf2i, wf2o = w_ffn2_in[l].astype(BF16), w_ffn2_out[l].astype(BF16)
        wi, wa, wb, wo = (w[l].astype(BF16) for w in (w_in, w_proj_a, w_proj_b, w_o))
        g1, b1, g2, b2, g3, b3 = (row(a[l]) for a in (ln1_g, ln1_b, ln2_g, ln2_b, ln3_g, ln3_b))
        lvg, lvb, bg = row(ln_v_g[l]), row(ln_v_b[l]), row(b_gate[l])
        gwid = GMLP_WIDTH // N_GROUPS
        bias_p = jnp.repeat(b_s[l].T, gwid, axis=1)
        scale_s = row(jnp.repeat(w_s[l, :, 0, 0], gwid))
        bias_s = row(jnp.repeat(b_s[l, :, 0], gwid))

        x1s = _ffn_ln(xs, wf1i, wf1o, g1, b1, alpha=alpha, tm=db)
        ya_s, q, k, v, hg_s, vg = _mixer(x1s, wi, lvg, lvb, cos_s, sin_s, scale_s, bias_s, spatial=False, tm=db)
        q3, k3, v3 = (a.reshape(db, 1, ATT_WIDTH) for a in (q, k, v))
        logits, idx, x1 = _sweep_ffn(cache_kt, page_table, q3, l, xp, wf1i, wf1o, g1, b1, alpha=alpha)
        k_s.append(k.reshape(db, dec_t, N_HEADS, HEAD_DIM))
        v_s.append(v.reshape(db, dec_t, N_HEADS, HEAD_DIM))
        cv_s.append(vg.reshape(db, dec_t, GMLP_WIDTH))

        ya, q, kt, vt, hg, kb, vb, ksum = _mixer(x1, wi, lvg, lvb, cos_p, sin_p, w_s[l], bias_p,
                                                 spatial=True, tm=BLOCK)
        yb = _moba_prompt(q, kb, vb, ksum, batch=batch, seq=seq)
        xp = _merge_ffn(x1, ya, yb, hg, bg, wa, wb, wo, g2, b2, wf2i, wf2o, g3, b3, alpha=alpha, tm=512)
        k_p.append(kt.reshape(batch, N_HEADS, HEAD_DIM, seq).transpose(0, 3, 1, 2))
        v_p.append(vt.reshape(batch, N_HEADS, HEAD_DIM, seq).transpose(0, 3, 1, 2))

        yb = _moba_sample(page_table, idx.reshape(db, TOP_K * N_HEADS), q3, k3, v3, logits, cache_vt, l)
        yb = yb.reshape(db, ATT_WIDTH).astype(BF16)
        xs = _merge_ffn(x1s, ya_s, yb, hg_s, bg, wa, wb, wo, g2, b2, wf2i, wf2o, g3, b3, alpha=alpha, tm=db)
    return (xp.reshape(batch, seq, d_model), xs.reshape(db, dec_t, d_model),
            jnp.stack(k_p), jnp.stack(v_p), jnp.stack(k_s), jnp.stack(v_s), jnp.stack(cv_s))
```

```python
import functools

import jax
import jax.numpy as jnp
from jax import lax
from jax.experimental import pallas as pl
from jax.experimental.pallas import tpu as pltpu

F32 = jnp.float32
BF16 = jnp.bfloat16

N_HEADS = 8
HEAD_DIM = 64
ATT_WIDTH = N_HEADS * HEAD_DIM
BLOCK = 256
TOP_K = 3
ROPE_THETA = 10000.0
CHUNK = 128
N_GROUPS = 8
GMLP_WIDTH = 512
LN_EPS = 1e-5
NEG_INF = -1e30
LANES = 128
FF_CHUNK = 256
VMEM_LIMIT = 56 * 1024 * 1024


def _resident(shape):
    zeros = (0,) * len(shape)
    return pl.BlockSpec(shape, lambda *_: zeros, pipeline_mode=pl.Buffered(1))


def _params(sem):
    return pltpu.CompilerParams(dimension_semantics=sem, vmem_limit_bytes=VMEM_LIMIT)


def _ln(y, g, b):
    mu = jnp.mean(y, axis=-1, keepdims=True)
    d = y - mu
    var = jnp.mean(d * d, axis=-1, keepdims=True)
    return d * lax.rsqrt(var + LN_EPS) * g + b


def _dot(a, b):
    return jnp.dot(a, b, preferred_element_type=F32)


def _dot_nt(a, b):
    return lax.dot_general(a, b, (((1,), (1,)), ((), ())), preferred_element_type=F32)


def _swiglu(xb, win_ref, wout_ref, chunks=None, acc=None):
    d_ff = wout_ref.shape[0]
    for c in (range(d_ff // FF_CHUNK) if chunks is None else chunks):
        lo, hi = c * FF_CHUNK, (c + 1) * FF_CHUNK
        a = _dot(xb, win_ref[:, lo:hi])
        g = _dot(xb, win_ref[:, d_ff + lo:d_ff + hi])
        act = (g * jax.nn.sigmoid(g) * a).astype(BF16)
        part = _dot(act, wout_ref[lo:hi, :])
        acc = part if acc is None else acc + part
    return acc


def _ffn_ln_kernel(x_ref, win_ref, wout_ref, g_ref, b_ref, o_ref, *, alpha):
    x = x_ref[...]
    h = _swiglu(x.astype(BF16), win_ref, wout_ref)
    o_ref[...] = _ln(alpha * x + 0.5 * h, g_ref[...], b_ref[...])


def _ffn_ln(x, win, wout, g, b, *, alpha, tm):
    n, d = x.shape
    row = lambda i: (i, 0)
    return pl.pallas_call(
        functools.partial(_ffn_ln_kernel, alpha=alpha),
        grid=(n // tm,),
        in_specs=[pl.BlockSpec((tm, d), row), _resident(win.shape), _resident(wout.shape),
                  _resident(g.shape), _resident(b.shape)],
        out_specs=pl.BlockSpec((tm, d), row),
        out_shape=jax.ShapeDtypeStruct((n, d), F32),
        compiler_params=_params(("parallel",)),
        name="ffn_ln",
    )(x, win, wout, g, b)


def _rope(x, cos, sin_signed):
    lane = lax.broadcasted_iota(jnp.int32, x.shape, 1)
    first = (lane % HEAD_DIM) < (HEAD_DIM // 2)
    half = HEAD_DIM // 2
    partner = jnp.where(first, pltpu.roll(x, x.shape[1] - half, 1), pltpu.roll(x, half, 1))
    return x * cos + partner * sin_signed


def _mixer_kernel(x_ref, w_ref, lng_ref, lnb_ref, cos_ref, sin_ref, ws_ref, bs_ref, *out_refs, spatial):
    if spatial:
        ya_ref, q_ref, k_ref, v_ref, hg_ref, kb_ref, vb_ref, ksum_ref = out_refs
    else:
        ya_ref, q_ref, k_ref, v_ref, hg_ref, vg_ref = out_refs
    xb = x_ref[...].astype(BF16)
    gw, aw = GMLP_WIDTH, ATT_WIDTH

    def proj(lo, hi):
        return _dot(xb, w_ref[:, lo:hi])

    u = jax.nn.gelu(proj(0, gw))
    vg = _ln(jax.nn.gelu(proj(gw, 2 * gw)), lng_ref[...], lnb_ref[...])
    if spatial:
        tm = x_ref.shape[0]
        r = lax.broadcasted_iota(jnp.int32, (CHUNK, CHUNK), 0)
        c = lax.broadcasted_iota(jnp.int32, (CHUNK, CHUNK), 1)
        tril = r >= c
        lane = lax.broadcasted_iota(jnp.int32, (CHUNK, LANES), 1)
        lo_half = lane < (LANES // 2)
        for p in range(N_GROUPS // 2):
            wcat = jnp.concatenate(
                [jnp.where(tril, ws_ref[2 * p], 0.0), jnp.where(tril, ws_ref[2 * p + 1], 0.0)],
                axis=1).astype(BF16)
            cols = slice(p * LANES, (p + 1) * LANES)
            for ch in range(tm // CHUNK):
                rows = slice(ch * CHUNK, (ch + 1) * CHUNK)
                vp = vg[rows, cols]
                vstack = jnp.concatenate(
                    [jnp.where(lo_half, vp, 0.0), jnp.where(lo_half, 0.0, vp)], axis=0).astype(BF16)
                mixed = _dot(wcat, vstack) + bs_ref[:, cols]
                ya_ref[rows, cols] = (u[rows, cols] * mixed).astype(ya_ref.dtype)
    else:
        ya_ref[...] = (u * (vg * ws_ref[...] + bs_ref[...])).astype(ya_ref.dtype)
        vg_ref[...] = vg
    cos, sin = cos_ref[...], sin_ref[...]
    q_ref[...] = _rope(proj(2 * gw, 2 * gw + aw), cos, sin)
    k = _rope(proj(2 * gw + aw, 2 * gw + 2 * aw), cos, sin)
    v = proj(2 * gw + 2 * aw, 2 * gw + 3 * aw)
    if spatial:
        k_ref[0] = k.T
        v_ref[0] = v.T
        kb_ref[...] = k.astype(BF16)
        vb_ref[...] = v.astype(BF16)
        for j in range(k.shape[0] // BLOCK):
            ksum_ref[j] = jnp.sum(k[j * BLOCK:(j + 1) * BLOCK], axis=0, keepdims=True)
    else:
        k_ref[...] = k
        v_ref[...] = v
    hg_ref[...] = proj(2 * gw + 3 * aw, w_ref.shape[1])


def _mixer(x, w, lng, lnb, cos, sin, ws, bs, *, spatial, tm):
    n, d = x.shape
    gate_w = w.shape[1] - 2 * GMLP_WIDTH - 3 * ATT_WIDTH
    row = lambda i: (i, 0)
    if spatial:
        n_tab = cos.shape[0] // tm
        tab = pl.BlockSpec((tm, ATT_WIDTH), lambda i: (i % n_tab, 0))
    else:
        tab = _resident(cos.shape)
    outs = [(GMLP_WIDTH, BF16), (ATT_WIDTH, F32), (ATT_WIDTH, F32), (ATT_WIDTH, F32), (gate_w, F32)]
    outs += [(ATT_WIDTH, BF16), (ATT_WIDTH, BF16)] if spatial else [(GMLP_WIDTH, F32)]
    out_specs = [pl.BlockSpec((tm, wd), row) for wd, _ in outs]
    out_shape = [jax.ShapeDtypeStruct((n, wd), dt) for wd, dt in outs]
    if spatial:
        assert tm % BLOCK == 0 and cos.shape[0] % tm == 0, "mixer tiles hold whole MoBA key blocks of one sequence"
        out_specs.append(pl.BlockSpec((tm // BLOCK, 1, ATT_WIDTH), lambda i: (i, 0, 0)))
        out_shape.append(jax.ShapeDtypeStruct((n // BLOCK, 1, ATT_WIDTH), F32))
        for kv in (2, 3):
            out_specs[kv] = pl.BlockSpec((1, ATT_WIDTH, tm), lambda i: (i // n_tab, 0, i % n_tab))
            out_shape[kv] = jax.ShapeDtypeStruct((n // cos.shape[0], ATT_WIDTH, cos.shape[0]), F32)
    return pl.pallas_call(
        functools.partial(_mixer_kernel, spatial=spatial),
        grid=(n // tm,),
        in_specs=[pl.BlockSpec((tm, d), row), _resident(w.shape), _resident(lng.shape), _resident(lnb.shape),
                  tab, tab, _resident(ws.shape), _resident(bs.shape)],
        out_specs=out_specs,
        out_shape=out_shape,
        compiler_params=_params(("parallel",)),
        name="mixer_prompt" if spatial else "mixer_sample",
    )(x, w, lng, lnb, cos, sin, ws, bs)


def _moba_prompt_kernel(q_ref, k_ref, v_ref, ksum_ref, o_ref, kme_ref, sel_ref, lhs_ref, s_ref, mrun_ref, acc_ref,
                        *, nb):
    qb = pl.program_id(1)
    nh = N_HEADS
    width = q_ref.shape[1]

    @pl.when(qb == 0)
    def _():
        km = ksum_ref[...].reshape(nb, width) * (1.0 / BLOCK)
        head_of_lane = lax.broadcasted_iota(jnp.int32, (nb, width), 1) // HEAD_DIM
        kme_ref[...] = jnp.concatenate(
            [jnp.where(head_of_lane == h, km, 0.0) for h in range(nh) for _ in range(2)], axis=0)
        lane_b = lax.broadcasted_iota(jnp.int32, (BLOCK, LANES), 1)
        for j in range(nb):
            sel_ref[j] = jnp.where(lane_b == j, 1.0, 0.0).astype(BF16)

    q = q_ref[...]
    s = lax.dot_general(q, kme_ref[...], (((1,), (1,)), ((), ())),
                        precision=lax.Precision.HIGHEST, preferred_element_type=F32)
    lane = lax.broadcasted_iota(jnp.int32, s.shape, 1)
    jb = lane % nb
    s = jnp.where(jb < qb, s, NEG_INF)
    rank = jnp.zeros(s.shape, jnp.int32)
    for r in range(1, nb):
        sr = pltpu.roll(s, r, 1)
        jr = (jb + (nb - r)) % nb
        ahead = (sr > s) | ((sr == s) & (jr < jb))
        rank = rank + ahead.astype(jnp.int32)
    bias = jnp.where((rank < min(TOP_K, nb)) & (jb < qb), 0.0, NEG_INF)

    lo_half = lax.broadcasted_iota(jnp.int32, (BLOCK, LANES), 1) < HEAD_DIM
    rr = lax.broadcasted_iota(jnp.int32, (BLOCK, BLOCK), 0)
    cc = lax.broadcasted_iota(jnp.int32, (BLOCK, BLOCK), 1)
    causal = rr >= cc
    zeros_b = jnp.zeros((BLOCK, LANES), BF16)
    ones_b = jnp.ones((BLOCK, LANES), BF16)

    def rows_of(j):
        return pl.ds(pl.multiple_of(j * BLOCK, BLOCK), BLOCK)

    def half_max(x):
        return jnp.maximum(x[:, :LANES], x[:, LANES:])

    pair_cols = [slice(hp * LANES, (hp + 1) * LANES) for hp in range(nh // 2)]

    for h in range(nh):
        qp = q[:, pair_cols[h // 2]] * (HEAD_DIM ** -0.5)
        qh = jnp.where(lo_half, qp, 0.0) if h % 2 == 0 else jnp.where(lo_half, 0.0, qp)
        bh = pltpu.roll(bias, (LANES - (2 * nb * h + nb)) % LANES, 1)
        bh = jnp.where(lane < nb, bh, 0.0)
        lhs_ref[h] = jnp.concatenate([qh, bh], axis=1).astype(BF16)

    for h in range(nh):
        k_own = jnp.concatenate([k_ref[rows_of(qb), pair_cols[h // 2]], zeros_b], axis=1)
        sc = jnp.where(causal, _dot_nt(lhs_ref[h], k_own), NEG_INF)
        s_ref[h, qb] = sc
        mrun_ref[h] = half_max(sc)

    def logits_step(j, carry):
        sel_j = sel_ref[j]
        for h in range(nh):
            k_j = jnp.concatenate([k_ref[rows_of(j), pair_cols[h // 2]], sel_j], axis=1)
            sc = _dot_nt(lhs_ref[h], k_j)
            s_ref[h, j] = sc
            mrun_ref[h] = jnp.maximum(mrun_ref[h], half_max(sc))
        return carry

    lax.fori_loop(0, qb, logits_step, 0)

    for h in range(nh):
        m = jnp.max(mrun_ref[h], axis=1, keepdims=True)
        mrun_ref[h] = jnp.broadcast_to(m, (BLOCK, LANES))
        acc_ref[h] = jnp.zeros(acc_ref.shape[1:], F32)

    def pv_step(j, carry):
        for h in range(nh):
            v_j = jnp.concatenate([v_ref[rows_of(j), pair_cols[h // 2]], ones_b], axis=1)
            m = mrun_ref[h]
            p = jnp.exp(s_ref[h, j] - jnp.concatenate([m, m], axis=1)).astype(BF16)
            acc_ref[h] += _dot(p, v_j)
        return carry

    lax.fori_loop(0, qb + 1, pv_step, 0)
    for hp in range(nh // 2):
        out = [acc_ref[2 * hp + t, :, :LANES] / acc_ref[2 * hp + t, :, LANES:] for t in range(2)]
        o_ref[:, pair_cols[hp]] = jnp.where(lo_half, out[0], out[1]).astype(o_ref.dtype)


def _moba_prompt(q, kb, vb, ksum, *, batch, seq):
    nb = seq // BLOCK
    assert nb * BLOCK == seq and 2 * nb * N_HEADS == LANES, "prompt MoBA sweep expects 8 key blocks of 256"
    width = q.shape[1]
    return pl.pallas_call(
        functools.partial(_moba_prompt_kernel, nb=nb),
        grid=(batch, nb),
        in_specs=[pl.BlockSpec((BLOCK, width), lambda b, i: (b * nb + i, 0)),
                  pl.BlockSpec((seq, width), lambda b, i: (b, 0)),
                  pl.BlockSpec((seq, width), lambda b, i: (b, 0)),
                  pl.BlockSpec((nb, 1, width), lambda b, i: (b, 0, 0))],
        out_specs=pl.BlockSpec((BLOCK, width), lambda b, i: (b * nb + i, 0)),
        out_shape=jax.ShapeDtypeStruct(q.shape, BF16),
        scratch_shapes=[pltpu.VMEM((2 * nb * N_HEADS, width), F32),
                        pltpu.VMEM((nb, BLOCK, LANES), BF16),
                        pltpu.VMEM((N_HEADS, BLOCK, 2 * LANES), BF16),
                        pltpu.VMEM((N_HEADS, nb, BLOCK, BLOCK), F32),
                        pltpu.VMEM((N_HEADS, BLOCK, LANES), F32),
                        pltpu.VMEM((N_HEADS, BLOCK, 2 * LANES), F32)],
        compiler_params=_params(("parallel", "arbitrary")),
        name="moba_prompt",
    )(q, kb, vb, ksum)


def _column_broadcast(row):
    return jnp.transpose(jnp.broadcast_to(row, (LANES, row.shape[1])))


def _per_head_sum(x):
    return jnp.sum(x.reshape(N_HEADS, HEAD_DIM, x.shape[1]), axis=1)


def _sweep_ffn_kernel(pt_ref, q_ref, ck_ref, x_ref, win_ref, wout_ref, g_ref, b_ref,
                      lg_ref, idx_ref, o_ref, kbuf, sem, part_ref, *, layer, ppb, alpha):
    b = pl.program_id(0)
    half, rows = kbuf.shape[1], kbuf.shape[-1]
    n_chunks = wout_ref.shape[0] // FF_CHUNK
    ffn_part = (range(0, n_chunks // 2), range(n_chunks // 2, n_chunks))
    sub = HEAD_DIM // 8

    def page_copy(seq, part, i):
        return pltpu.make_async_copy(ck_ref.at[layer, pt_ref[seq, part * half + i]], kbuf.at[part, i], sem.at[part])

    @pl.when(b == 0)
    def _():
        for part in range(2):
            for i in range(half):
                page_copy(0, part, i).start()

    qcol = _column_broadcast(q_ref[0] * (HEAD_DIM ** -0.5))
    x = x_ref[...]
    xb = x.astype(BF16)
    ffn = None
    scores = []
    for part in range(2):
        for i in range(half):
            page_copy(b, part, i).wait()
        for h in range(N_HEADS):
            qh = qcol[h * HEAD_DIM:(h + 1) * HEAD_DIM]
            for i in range(half):
                part_ref[i, h] = jnp.sum((kbuf[part, i, h] * qh).reshape(sub, 8, rows), axis=0)
        for j in range(half // ppb):
            tot = None
            for p in range(ppb):
                i = j * ppb + p
                lg = jnp.sum(part_ref[i], axis=1)
                lg_ref[0, :, (part * half + i) * rows:(part * half + i + 1) * rows] = lg
                tot = lg if tot is None else tot + lg
            scores.append(jnp.sum(tot, axis=1, keepdims=True) * (HEAD_DIM ** 0.5 / BLOCK))

        @pl.when(b + 1 < pl.num_programs(0))
        def _():
            for i in range(half):
                page_copy(b + 1, part, i).start()

        ffn = _swiglu(xb, win_ref, wout_ref, ffn_part[part], ffn)

    o_ref[...] = _ln(alpha * x + 0.5 * ffn, g_ref[...], b_ref[...])
    nblk = len(scores)
    picks = []
    for _ in range(TOP_K):
        mx = functools.reduce(jnp.maximum, scores)
        ix = jnp.full(mx.shape, nblk, jnp.int32)
        for j in reversed(range(nblk)):
            ix = jnp.where(scores[j] == mx, j, ix)
        picks.append(ix)
        scores = [jnp.where(ix == j, -jnp.inf, sc) for j, sc in enumerate(scores)]
    idx_ref[0] = jnp.stack(picks, axis=0)


def _sweep_ffn(cache_kt, page_table, q, layer, x, win, wout, g, b, *, alpha):
    _, _, nh, hd, rows = cache_kt.shape
    db, n_pages = page_table.shape
    n, d = x.shape
    ppb = BLOCK // rows
    tm = n // db
    assert n_pages // ppb >= TOP_K and n_pages % (2 * ppb) == 0 and hd % 8 == 0
    assert tm * db == n and tm % 8 == 0, "one prompt row tile per sample sequence"
    const = lambda shape: pl.BlockSpec(shape, lambda i, pt: (0,) * len(shape), pipeline_mode=pl.Buffered(1))
    return pl.pallas_call(
        functools.partial(_sweep_ffn_kernel, layer=layer, ppb=ppb, alpha=alpha),
        grid_spec=pltpu.PrefetchScalarGridSpec(
            num_scalar_prefetch=1,
            grid=(db,),
            in_specs=[pl.BlockSpec((1, 1, nh * hd), lambda i, pt: (i, 0, 0)), pl.BlockSpec(memory_space=pl.ANY),
                      pl.BlockSpec((tm, d), lambda i, pt: (i, 0)),
                      const(win.shape), const(wout.shape), const(g.shape), const(b.shape)],
            out_specs=[pl.BlockSpec((1, nh, n_pages * rows), lambda i, pt: (i, 0, 0)),
                       pl.BlockSpec((1, TOP_K, nh, 1), lambda i, pt: (i, 0, 0, 0)),
                       pl.BlockSpec((tm, d), lambda i, pt: (i, 0))],
            scratch_shapes=[pltpu.VMEM((2, n_pages // 2, nh, hd, rows), F32), pltpu.SemaphoreType.DMA((2,)),
                            pltpu.VMEM((n_pages // 2, nh, 8, rows), F32)],
        ),
        out_shape=[jax.ShapeDtypeStruct((db, nh, n_pages * rows), F32),
                   jax.ShapeDtypeStruct((db, TOP_K, nh, 1), jnp.int32),
                   jax.ShapeDtypeStruct((n, d), F32)],
        compiler_params=_params(("arbitrary",)),
        name="sweep_ffn",
    )(page_table, q, cache_kt, x, win, wout, g, b)


def _moba_sample_kernel(pt_ref, idx_ref, q_ref, kn_ref, vn_ref, lg_ref, cv_ref, o_ref, vbuf, sem, *, layer, ppb):
    b = pl.program_id(0)
    slot = b % 2
    rows = cv_ref.shape[-1]
    slabs = [(h, r, pg) for h in range(N_HEADS) for r in range(TOP_K) for pg in range(ppb)]

    def slab_copy(seq, buf, h, r, pg):
        blk = idx_ref[seq, r * N_HEADS + h]
        phys = pt_ref[seq, blk * ppb + pg]
        return pltpu.make_async_copy(cv_ref.at[layer, phys, h], vbuf.at[buf, h, r * ppb + pg], sem.at[buf])

    @pl.when(b == 0)
    def _():
        for s in slabs:
            slab_copy(0, 0, *s).start()

    @pl.when(b + 1 < pl.num_programs(0))
    def _():
        for s in slabs:
            slab_copy(b + 1, 1 - slot, *s).start()

    s_sel = []
    for r in range(TOP_K):
        per_head = []
        for h in range(N_HEADS):
            start = pl.multiple_of(idx_ref[b, r * N_HEADS + h] * BLOCK, BLOCK)
            per_head.append(lg_ref[0, h:h + 1, pl.ds(start, BLOCK)])
        s_sel.append(jnp.concatenate(per_head, axis=0))
    qcol = _column_broadcast(q_ref[0] * (HEAD_DIM ** -0.5))
    s_self = _per_head_sum(qcol * _column_broadcast(kn_ref[0]))[:, :1]
    m = s_self
    for s in s_sel:
        m = jnp.maximum(m, jnp.max(s, axis=1, keepdims=True))
    p_sel = [jnp.exp(s - m) for s in s_sel]
    p_self = jnp.exp(s_self - m)
    denom = p_self
    for p in p_sel:
        denom = denom + jnp.sum(p, axis=1, keepdims=True)

    for s in slabs:
        slab_copy(b, slot, *s).wait()

    vn_col = _column_broadcast(vn_ref[0])[:, :1]
    out_cols = []
    for h in range(N_HEADS):
        acc = None
        for r in range(TOP_K):
            for pg in range(ppb):
                term = vbuf[slot, h, r * ppb + pg] * p_sel[r][h:h + 1, pg * rows:(pg + 1) * rows]
                acc = term if acc is None else acc + term
        num = jnp.sum(acc, axis=1, keepdims=True) + p_self[h:h + 1, :] * vn_col[h * HEAD_DIM:(h + 1) * HEAD_DIM, :]
        out_cols.append(num / denom[h:h + 1, :])
    out_col = jnp.concatenate(out_cols, axis=0)
    o_ref[0] = jnp.transpose(jnp.broadcast_to(out_col, (out_col.shape[0], LANES)))[:1, :]


def _moba_sample(page_table, idx, q, k_new, v_new, logits, cache_vt, layer):
    db, _, width = q.shape
    _, _, nh, hd, rows = cache_vt.shape
    ppb = BLOCK // rows
    seq_spec = pl.BlockSpec((1, 1, width), lambda b, pt, ix: (b, 0, 0))
    return pl.pallas_call(
        functools.partial(_moba_sample_kernel, layer=layer, ppb=ppb),
        grid_spec=pltpu.PrefetchScalarGridSpec(
            num_scalar_prefetch=2,
            grid=(db,),
            in_specs=[seq_spec, seq_spec, seq_spec,
                      pl.BlockSpec((1, nh, logits.shape[2]), lambda b, pt, ix: (b, 0, 0)),
                      pl.BlockSpec(memory_space=pl.ANY)],
            out_specs=seq_spec,
            scratch_shapes=[pltpu.VMEM((2, nh, TOP_K * ppb, hd, rows), F32), pltpu.SemaphoreType.DMA((2,))],
        ),
        out_shape=jax.ShapeDtypeStruct((db, 1, width), F32),
        compiler_params=_params(("arbitrary",)),
        name="moba_sample",
    )(page_table, idx, q, k_new, v_new, logits, cache_vt)


def _merge_ffn_kernel(x_ref, ya_ref, yb_ref, hg_ref, bg_ref, wa_ref, wb_ref, wo_ref, g2_ref, b2_ref,
                      win_ref, wout_ref, g3_ref, b3_ref, o_ref, *, alpha):
    x = x_ref[...]
    dm = x.shape[1]
    gates = jax.nn.sigmoid(hg_ref[...] + bg_ref[...])
    mix_in = gates[:, :dm] * _dot(ya_ref[...], wa_ref[...]) + gates[:, dm:] * _dot(yb_ref[...], wb_ref[...])
    x2 = _ln(alpha * x + _dot(mix_in.astype(BF16), wo_ref[...]), g2_ref[...], b2_ref[...])
    h = _swiglu(x2.astype(BF16), win_ref, wout_ref)
    o_ref[...] = _ln(alpha * x2 + 0.5 * h, g3_ref[...], b3_ref[...])


def _merge_ffn(x, ya, yb, hg, bg, wa, wb, wo, g2, b2, win, wout, g3, b3, *, alpha, tm):
    n, d = x.shape
    row = lambda i: (i, 0)
    tiles = [x, ya, yb, hg]
    consts = [bg, wa, wb, wo, g2, b2, win, wout, g3, b3]
    return pl.pallas_call(
        functools.partial(_merge_ffn_kernel, alpha=alpha),
        grid=(n // tm,),
        in_specs=[pl.BlockSpec((tm, a.shape[1]), row) for a in tiles] + [_resident(a.shape) for a in consts],
        out_specs=pl.BlockSpec((tm, d), row),
        out_shape=jax.ShapeDtypeStruct((n, d), F32),
        compiler_params=_params(("parallel",)),
        name="merge_ffn",
    )(*tiles, *consts)


def _rope_tables(pos):
    half = HEAD_DIM // 2
    inv = ROPE_THETA ** (-jnp.arange(half, dtype=F32) / half)
    ang = pos.astype(F32)[:, None] * inv[None, :]
    cos, sin = jnp.cos(ang), jnp.sin(ang)
    return (jnp.tile(jnp.concatenate([cos, cos], -1), (1, N_HEADS)),
            jnp.tile(jnp.concatenate([-sin, sin], -1), (1, N_HEADS)))


def kernel(x_prompt, x_sample, cache_k, cache_v, page_table, w_ffn1_in, w_ffn1_out, ln1_g, ln1_b, w_in, w_s, b_s,
           ln_v_g, ln_v_b, b_gate, w_proj_a, w_proj_b, w_o, ln2_g, ln2_b, w_ffn2_in, w_ffn2_out, ln3_g, ln3_b):
    batch, seq, d_model = x_prompt.shape
    db, dec_t, _ = x_sample.shape
    depth = w_in.shape[0]
    assert dec_t == 1, "sample group decodes one token per sequence"
    assert seq % BLOCK == 0 and N_GROUPS * 2 * (LANES // 2) == 2 * GMLP_WIDTH
    assert cache_k.shape[3:] == (N_HEADS, HEAD_DIM) and BLOCK % cache_k.shape[2] == 0
    past_len = page_table.shape[1] * cache_k.shape[2]
    alpha = (2 * depth) ** 0.25
    cos_p, sin_p = _rope_tables(jnp.arange(seq))
    cos_s, sin_s = _rope_tables(past_len + jnp.arange(dec_t))
    row = lambda a: a.reshape(1, -1)
    cache_kt = jnp.transpose(cache_k, (0, 1, 3, 4, 2))
    cache_vt = jnp.transpose(cache_v, (0, 1, 3, 4, 2))

    xp = x_prompt.reshape(batch * seq, d_model)
    xs = x_sample.reshape(db * dec_t, d_model)
    k_p, v_p, k_s, v_s, cv_s = [], [], [], [], []
    for l in range(depth):
        wf1i, wf1o = w_ffn1_in[l].astype(BF16), w_ffn1_out[l].astype(BF16)
        wf2i, wf2o = w_ffn2_in[l].astype(BF16), w_ffn2_out[l].astype(BF16)
        wi, wa, wb, wo = (w[l].astype(BF16) for w in (w_in, w_proj_a, w_proj_b, w_o))
        g1, b1, g2, b2, g3, b3 = (row(a[l]) for a in (ln1_g, ln1_b, ln2_g, ln2_b, ln3_g, ln3_b))
        lvg, lvb, bg = row(ln_v_g[l]), row(ln_v_b[l]), row(b_gate[l])
        gwid = GMLP_WIDTH // N_GROUPS
        bias_p = jnp.repeat(b_s[l].T, gwid, axis=1)
        scale_s = row(jnp.repeat(w_s[l, :, 0, 0], gwid))
        bias_s = row(jnp.repeat(b_s[l, :, 0], gwid))

        x1s = _ffn_ln(xs, wf1i, wf1o, g1, b1, alpha=alpha, tm=db)
        ya_s, q, k, v, hg_s, vg = _mixer(x1s, wi, lvg, lvb, cos_s, sin_s, scale_s, bias_s, spatial=False, tm=db)
        q3, k3, v3 = (a.reshape(db, 1, ATT_WIDTH) for a in (q, k, v))
        logits, idx, x1 = _sweep_ffn(cache_kt, page_table, q3, l, xp, wf1i, wf1o, g1, b1, alpha=alpha)
        k_s.append(k.reshape(db, dec_t, N_HEADS, HEAD_DIM))
        v_s.append(v.reshape(db, dec_t, N_HEADS, HEAD_DIM))
        cv_s.append(vg.reshape(db, dec_t, GMLP_WIDTH))

        ya, q, kt, vt, hg, kb, vb, ksum = _mixer(x1, wi, lvg, lvb, cos_p, sin_p, w_s[l], bias_p,
                                                 spatial=True, tm=2 * BLOCK)
        yb = _moba_prompt(q, kb, vb, ksum, batch=batch, seq=seq)
        xp = _merge_ffn(x1, ya, yb, hg, bg, wa, wb, wo, g2, b2, wf2i, wf2o, g3, b3, alpha=alpha, tm=512)
        k_p.append(kt.reshape(batch, N_HEADS, HEAD_DIM, seq).transpose(0, 3, 1, 2))
        v_p.append(vt.reshape(batch, N_HEADS, HEAD_DIM, seq).transpose(0, 3, 1, 2))

        yb = _moba_sample(page_table, idx.reshape(db, TOP_K * N_HEADS), q3, k3, v3, logits, cache_vt, l)
        yb = yb.reshape(db, ATT_WIDTH).astype(BF16)
        xs = _merge_ffn(x1s, ya_s, yb, hg_s, bg, wa, wb, wo, g2, b2, wf2i, wf2o, g3, b3, alpha=alpha, tm=db)
    return (xp.reshape(batch, seq, d_model), xs.reshape(db, dec_t, d_model),
            jnp.stack(k_p), jnp.stack(v_p), jnp.stack(k_s), jnp.stack(v_s), jnp.stack(cv_s))
```

```python
import functools

import jax
import jax.numpy as jnp
from jax import lax
from jax.experimental import pallas as pl
from jax.experimental.pallas import tpu as pltpu

F32 = jnp.float32
BF16 = jnp.bfloat16

N_HEADS = 8
HEAD_DIM = 64
ATT_WIDTH = N_HEADS * HEAD_DIM
BLOCK = 256
TOP_K = 3
ROPE_THETA = 10000.0
CHUNK = 128
N_GROUPS = 8
GMLP_WIDTH = 512
LN_EPS = 1e-5
NEG_INF = -1e30
LANES = 128
FF_CHUNK = 256
VMEM_LIMIT = 56 * 1024 * 1024


def _resident(shape):
    zeros = (0,) * len(shape)
    return pl.BlockSpec(shape, lambda *_: zeros, pipeline_mode=pl.Buffered(1))


def _params(sem):
    return pltpu.CompilerParams(dimension_semantics=sem, vmem_limit_bytes=VMEM_LIMIT)


def _ln(y, g, b):
    mu = jnp.mean(y, axis=-1, keepdims=True)
    d = y - mu
    var = jnp.mean(d * d, axis=-1, keepdims=True)
    return d * lax.rsqrt(var + LN_EPS) * g + b


def _dot(a, b):
    return jnp.dot(a, b, preferred_element_type=F32)


def _dot_nt(a, b):
    return lax.dot_general(a, b, (((1,), (1,)), ((), ())), preferred_element_type=F32)


def _swiglu(xb, win_ref, wout_ref, chunks=None, acc=None):
    d_ff = wout_ref.shape[0]
    for c in (range(d_ff // FF_CHUNK) if chunks is None else chunks):
        lo, hi = c * FF_CHUNK, (c + 1) * FF_CHUNK
        a = _dot(xb, win_ref[:, lo:hi])
        g = _dot(xb, win_ref[:, d_ff + lo:d_ff + hi])
        act = (g * jax.nn.sigmoid(g) * a).astype(BF16)
        part = _dot(act, wout_ref[lo:hi, :])
        acc = part if acc is None else acc + part
    return acc


def _ffn_ln_kernel(x_ref, win_ref, wout_ref, g_ref, b_ref, o_ref, *, alpha):
    x = x_ref[...]
    h = _swiglu(x.astype(BF16), win_ref, wout_ref)
    o_ref[...] = _ln(alpha * x + 0.5 * h, g_ref[...], b_ref[...])


def _ffn_ln(x, win, wout, g, b, *, alpha, tm):
    n, d = x.shape
    row = lambda i: (i, 0)
    return pl.pallas_call(
        functools.partial(_ffn_ln_kernel, alpha=alpha),
        grid=(n // tm,),
        in_specs=[pl.BlockSpec((tm, d), row), _resident(win.shape), _resident(wout.shape),
                  _resident(g.shape), _resident(b.shape)],
        out_specs=pl.BlockSpec((tm, d), row),
        out_shape=jax.ShapeDtypeStruct((n, d), F32),
        compiler_params=_params(("parallel",)),
        name="ffn_ln",
    )(x, win, wout, g, b)


def _rope(x, cos, sin_signed):
    lane = lax.broadcasted_iota(jnp.int32, x.shape, 1)
    first = (lane % HEAD_DIM) < (HEAD_DIM // 2)
    half = HEAD_DIM // 2
    partner = jnp.where(first, pltpu.roll(x, x.shape[1] - half, 1), pltpu.roll(x, half, 1))
    return x * cos + partner * sin_signed


def _mixer_kernel(x_ref, w_ref, lng_ref, lnb_ref, cos_ref, sin_ref, ws_ref, bs_ref, *out_refs, spatial):
    if spatial:
        ya_ref, q_ref, k_ref, v_ref, hg_ref, kb_ref, vb_ref, ksum_ref = out_refs
    else:
        ya_ref, q_ref, k_ref, v_ref, hg_ref, vg_ref = out_refs
    xb = x_ref[...].astype(BF16)
    gw, aw = GMLP_WIDTH, ATT_WIDTH

    def proj(lo, hi):
        return _dot(xb, w_ref[:, lo:hi])

    u = jax.nn.gelu(proj(0, gw))
    vg = _ln(jax.nn.gelu(proj(gw, 2 * gw)), lng_ref[...], lnb_ref[...])
    if spatial:
        tm = x_ref.shape[0]
        r = lax.broadcasted_iota(jnp.int32, (CHUNK, CHUNK), 0)
        c = lax.broadcasted_iota(jnp.int32, (CHUNK, CHUNK), 1)
        tril = r >= c
        lane = lax.broadcasted_iota(jnp.int32, (CHUNK, LANES), 1)
        lo_half = lane < (LANES // 2)
        for p in range(N_GROUPS // 2):
            wcat = jnp.concatenate(
                [jnp.where(tril, ws_ref[2 * p], 0.0), jnp.where(tril, ws_ref[2 * p + 1], 0.0)],
                axis=1).astype(BF16)
            cols = slice(p * LANES, (p + 1) * LANES)
            for ch in range(tm // CHUNK):
                rows = slice(ch * CHUNK, (ch + 1) * CHUNK)
                vp = vg[rows, cols]
                vstack = jnp.concatenate(
                    [jnp.where(lo_half, vp, 0.0), jnp.where(lo_half, 0.0, vp)], axis=0).astype(BF16)
                mixed = _dot(wcat, vstack) + bs_ref[:, cols]
                ya_ref[rows, cols] = (u[rows, cols] * mixed).astype(ya_ref.dtype)
    else:
        ya_ref[...] = (u * (vg * ws_ref[...] + bs_ref[...])).astype(ya_ref.dtype)
        vg_ref[...] = vg
    cos, sin = cos_ref[...], sin_ref[...]
    q_ref[...] = _rope(proj(2 * gw, 2 * gw + aw), cos, sin)
    k = _rope(proj(2 * gw + aw, 2 * gw + 2 * aw), cos, sin)
    v = proj(2 * gw + 2 * aw, 2 * gw + 3 * aw)
    if spatial:
        k_ref[0] = k.T
        v_ref[0] = v.T
        kb_ref[...] = k.astype(BF16)
        vb_ref[...] = v.astype(BF16)
        for j in range(k.shape[0] // BLOCK):
            ksum_ref[j] = jnp.sum(k[j * BLOCK:(j + 1) * BLOCK], axis=0, keepdims=True)
    else:
        k_ref[...] = k
        v_ref[...] = v
    hg_ref[...] = proj(2 * gw + 3 * aw, w_ref.shape[1])


def _mixer(x, w, lng, lnb, cos, sin, ws, bs, *, spatial, tm):
    n, d = x.shape
    gate_w = w.shape[1] - 2 * GMLP_WIDTH - 3 * ATT_WIDTH
    row = lambda i: (i, 0)
    if spatial:
        n_tab = cos.shape[0] // tm
        tab = pl.BlockSpec((tm, ATT_WIDTH), lambda i: (i % n_tab, 0))
    else:
        tab = _resident(cos.shape)
    outs = [(GMLP_WIDTH, BF16), (ATT_WIDTH, F32), (ATT_WIDTH, F32), (ATT_WIDTH, F32), (gate_w, F32)]
    outs += [(ATT_WIDTH, BF16), (ATT_WIDTH, BF16)] if spatial else [(GMLP_WIDTH, F32)]
    out_specs = [pl.BlockSpec((tm, wd), row) for wd, _ in outs]
    out_shape = [jax.ShapeDtypeStruct((n, wd), dt) for wd, dt in outs]
    if spatial:
        assert tm % BLOCK == 0 and cos.shape[0] % tm == 0, "mixer tiles hold whole MoBA key blocks of one sequence"
        out_specs.append(pl.BlockSpec((tm // BLOCK, 1, ATT_WIDTH), lambda i: (i, 0, 0)))
        out_shape.append(jax.ShapeDtypeStruct((n // BLOCK, 1, ATT_WIDTH), F32))
        for kv in (2, 3):
            out_specs[kv] = pl.BlockSpec((1, ATT_WIDTH, tm), lambda i: (i // n_tab, 0, i % n_tab))
            out_shape[kv] = jax.ShapeDtypeStruct((n // cos.shape[0], ATT_WIDTH, cos.shape[0]), F32)
    return pl.pallas_call(
        functools.partial(_mixer_kernel, spatial=spatial),
        grid=(n // tm,),
        in_specs=[pl.BlockSpec((tm, d), row), _resident(w.shape), _resident(lng.shape), _resident(lnb.shape),
                  tab, tab, _resident(ws.shape), _resident(bs.shape)],
        out_specs=out_specs,
        out_shape=out_shape,
        compiler_params=_params(("parallel",)),
        name="mixer_prompt" if spatial else "mixer_sample",
    )(x, w, lng, lnb, cos, sin, ws, bs)


def _moba_prompt_kernel(q_ref, k_ref, v_ref, ksum_ref, o_ref, kme_ref, sel_ref, lhs_ref, s_ref, mrun_ref, acc_ref,
                        *, nb):
    qb = pl.program_id(1)
    nh = N_HEADS
    width = q_ref.shape[1]

    @pl.when(qb == 0)
    def _():
        km = ksum_ref[...].reshape(nb, width) * (1.0 / BLOCK)
        head_of_lane = lax.broadcasted_iota(jnp.int32, (nb, width), 1) // HEAD_DIM
        kme_ref[...] = jnp.concatenate(
            [jnp.where(head_of_lane == h, km, 0.0) for h in range(nh) for _ in range(2)], axis=0)
        lane_b = lax.broadcasted_iota(jnp.int32, (BLOCK, LANES), 1)
        for j in range(nb):
            sel_ref[j] = jnp.where(lane_b == j, 1.0, 0.0).astype(BF16)

    q = q_ref[...]
    s = lax.dot_general(q, kme_ref[...], (((1,), (1,)), ((), ())),
                        precision=lax.Precision.HIGHEST, preferred_element_type=F32)
    lane = lax.broadcasted_iota(jnp.int32, s.shape, 1)
    jb = lane % nb
    s = jnp.where(jb < qb, s, NEG_INF)
    rank = jnp.zeros(s.shape, jnp.int32)
    for r in range(1, nb):
        sr = pltpu.roll(s, r, 1)
        jr = (jb + (nb - r)) % nb
        ahead = (sr > s) | ((sr == s) & (jr < jb))
        rank = rank + ahead.astype(jnp.int32)
    bias = jnp.where((rank < min(TOP_K, nb)) & (jb < qb), 0.0, NEG_INF)

    lo_half = lax.broadcasted_iota(jnp.int32, (BLOCK, LANES), 1) < HEAD_DIM
    rr = lax.broadcasted_iota(jnp.int32, (BLOCK, BLOCK), 0)
    cc = lax.broadcasted_iota(jnp.int32, (BLOCK, BLOCK), 1)
    causal = rr >= cc
    zeros_b = jnp.zeros((BLOCK, LANES), BF16)
    ones_b = jnp.ones((BLOCK, LANES), BF16)

    def rows_of(j):
        return pl.ds(pl.multiple_of(j * BLOCK, BLOCK), BLOCK)

    def half_max(x):
        return jnp.maximum(x[:, :LANES], x[:, LANES:])

    pair_cols = [slice(hp * LANES, (hp + 1) * LANES) for hp in range(nh // 2)]

    for h in range(nh):
        qp = q[:, pair_cols[h // 2]] * (HEAD_DIM ** -0.5)
        qh = jnp.where(lo_half, qp, 0.0) if h % 2 == 0 else jnp.where(lo_half, 0.0, qp)
        bh = pltpu.roll(bias, (LANES - (2 * nb * h + nb)) % LANES, 1)
        bh = jnp.where(lane < nb, bh, 0.0)
        lhs_ref[h] = jnp.concatenate([qh, bh], axis=1).astype(BF16)

    for h in range(nh):
        k_own = jnp.concatenate([k_ref[rows_of(qb), pair_cols[h // 2]], zeros_b], axis=1)
        sc = jnp.where(causal, _dot_nt(lhs_ref[h], k_own), NEG_INF)
        s_ref[h, qb] = sc
        mrun_ref[h] = half_max(sc)

    def logits_step(j, carry):
        sel_j = sel_ref[j]
        for h in range(nh):
            k_j = jnp.concatenate([k_ref[rows_of(j), pair_cols[h // 2]], sel_j], axis=1)
            sc = _dot_nt(lhs_ref[h], k_j)
            s_ref[h, j] = sc
            mrun_ref[h] = jnp.maximum(mrun_ref[h], half_max(sc))
        return carry

    lax.fori_loop(0, qb, logits_step, 0)

    for h in range(nh):
        m = jnp.max(mrun_ref[h], axis=1, keepdims=True)
        mrun_ref[h] = jnp.broadcast_to(m, (BLOCK, LANES))
        acc_ref[h] = jnp.zeros(acc_ref.shape[1:], F32)

    def pv_step(j, carry):
        for h in range(nh):
            v_j = jnp.concatenate([v_ref[rows_of(j), pair_cols[h // 2]], ones_b], axis=1)
            m = mrun_ref[h]
            p = jnp.exp(s_ref[h, j] - jnp.concatenate([m, m], axis=1)).astype(BF16)
            acc_ref[h] += _dot(p, v_j)
        return carry

    lax.fori_loop(0, qb + 1, pv_step, 0)
    for hp in range(nh // 2):
        out = [acc_ref[2 * hp + t, :, :LANES] / acc_ref[2 * hp + t, :, LANES:] for t in range(2)]
        o_ref[:, pair_cols[hp]] = jnp.where(lo_half, out[0], out[1]).astype(o_ref.dtype)


def _moba_prompt(q, kb, vb, ksum, *, batch, seq):
    nb = seq // BLOCK
    assert nb * BLOCK == seq and 2 * nb * N_HEADS == LANES, "prompt MoBA sweep expects 8 key blocks of 256"
    width = q.shape[1]
    return pl.pallas_call(
        functools.partial(_moba_prompt_kernel, nb=nb),
        grid=(batch, nb),
        in_specs=[pl.BlockSpec((BLOCK, width), lambda b, i: (b * nb + i, 0)),
                  pl.BlockSpec((seq, width), lambda b, i: (b, 0)),
                  pl.BlockSpec((seq, width), lambda b, i: (b, 0)),
                  pl.BlockSpec((nb, 1, width), lambda b, i: (b, 0, 0))],
        out_specs=pl.BlockSpec((BLOCK, width), lambda b, i: (b * nb + i, 0)),
        out_shape=jax.ShapeDtypeStruct(q.shape, BF16),
        scratch_shapes=[pltpu.VMEM((2 * nb * N_HEADS, width), F32),
                        pltpu.VMEM((nb, BLOCK, LANES), BF16),
                        pltpu.VMEM((N_HEADS, BLOCK, 2 * LANES), BF16),
                        pltpu.VMEM((N_HEADS, nb, BLOCK, BLOCK), F32),
                        pltpu.VMEM((N_HEADS, BLOCK, LANES), F32),
                        pltpu.VMEM((N_HEADS, BLOCK, 2 * LANES), F32)],
        compiler_params=_params(("parallel", "arbitrary")),
        name="moba_prompt",
    )(q, kb, vb, ksum)


def _column_broadcast(row):
    return jnp.transpose(jnp.broadcast_to(row, (LANES, row.shape[1])))


def _per_head_sum(x):
    return jnp.sum(x.reshape(N_HEADS, HEAD_DIM, x.shape[1]), axis=1)


def _sweep_ffn_kernel(pt_ref, q_ref, ck_ref, x_ref, win_ref, wout_ref, g_ref, b_ref,
                      lg_ref, idx_ref, o_ref, kbuf, sem, part_ref, *, layer, ppb, alpha):
    b = pl.program_id(0)
    half, rows = kbuf.shape[1], kbuf.shape[-1]
    n_chunks = wout_ref.shape[0] // FF_CHUNK
    ffn_part = (range(0, n_chunks // 2), range(n_chunks // 2, n_chunks))
    sub = HEAD_DIM // 8

    def page_copy(seq, part, i):
        return pltpu.make_async_copy(ck_ref.at[layer, pt_ref[seq, part * half + i]], kbuf.at[part, i], sem.at[part])

    @pl.when(b == 0)
    def _():
        for part in range(2):
            for i in range(half):
                page_copy(0, part, i).start()

    qcol = _column_broadcast(q_ref[0] * (HEAD_DIM ** -0.5))
    x = x_ref[...]
    xb = x.astype(BF16)
    ffn = None
    scores = []
    for part in range(2):
        for i in range(half):
            page_copy(b, part, i).wait()
        for h in range(N_HEADS):
            qh = qcol[h * HEAD_DIM:(h + 1) * HEAD_DIM]
            for i in range(half):
                part_ref[i, h] = jnp.sum((kbuf[part, i, h] * qh).reshape(sub, 8, rows), axis=0)
        for j in range(half // ppb):
            tot = None
            for p in range(ppb):
                i = j * ppb + p
                lg = jnp.sum(part_ref[i], axis=1)
                lg_ref[0, :, (part * half + i) * rows:(part * half + i + 1) * rows] = lg
                tot = lg if tot is None else tot + lg
            scores.append(jnp.sum(tot, axis=1, keepdims=True) * (HEAD_DIM ** 0.5 / BLOCK))

        @pl.when(b + 1 < pl.num_programs(0))
        def _():
            for i in range(half):
                page_copy(b + 1, part, i).start()

        ffn = _swiglu(xb, win_ref, wout_ref, ffn_part[part], ffn)

    o_ref[...] = _ln(alpha * x + 0.5 * ffn, g_ref[...], b_ref[...])
    nblk = len(scores)
    picks = []
    for _ in range(TOP_K):
        mx = functools.reduce(jnp.maximum, scores)
        ix = jnp.full(mx.shape, nblk, jnp.int32)
        for j in reversed(range(nblk)):
            ix = jnp.where(scores[j] == mx, j, ix)
        picks.append(ix)
        scores = [jnp.where(ix == j, -jnp.inf, sc) for j, sc in enumerate(scores)]
    idx_ref[0] = jnp.stack(picks, axis=0)


def _sweep_ffn(cache_kt, page_table, q, layer, x, win, wout, g, b, *, alpha):
    _, _, nh, hd, rows = cache_kt.shape
    db, n_pages = page_table.shape
    n, d = x.shape
    ppb = BLOCK // rows
    tm = n // db
    assert n_pages // ppb >= TOP_K and n_pages % (2 * ppb) == 0 and hd % 8 == 0
    assert tm * db == n and tm % 8 == 0, "one prompt row tile per sample sequence"
    const = lambda shape: pl.BlockSpec(shape, lambda i, pt: (0,) * len(shape), pipeline_mode=pl.Buffered(1))
    return pl.pallas_call(
        functools.partial(_sweep_ffn_kernel, layer=layer, ppb=ppb, alpha=alpha),
        grid_spec=pltpu.PrefetchScalarGridSpec(
            num_scalar_prefetch=1,
            grid=(db,),
            in_specs=[pl.BlockSpec((1, 1, nh * hd), lambda i, pt: (i, 0, 0)), pl.BlockSpec(memory_space=pl.ANY),
                      pl.BlockSpec((tm, d), lambda i, pt: (i, 0)),
                      const(win.shape), const(wout.shape), const(g.shape), const(b.shape)],
            out_specs=[pl.BlockSpec((1, nh, n_pages * rows), lambda i, pt: (i, 0, 0)),
                       pl.BlockSpec((1, TOP_K, nh, 1), lambda i, pt: (i, 0, 0, 0)),
                       pl.BlockSpec((tm, d), lambda i, pt: (i, 0))],
            scratch_shapes=[pltpu.VMEM((2, n_pages // 2, nh, hd, rows), F32), pltpu.SemaphoreType.DMA((2,)),
                            pltpu.VMEM((n_pages // 2, nh, 8, rows), F32)],
        ),
        out_shape=[jax.ShapeDtypeStruct((db, nh, n_pages * rows), F32),
                   jax.ShapeDtypeStruct((db, TOP_K, nh, 1), jnp.int32),
                   jax.ShapeDtypeStruct((n, d), F32)],
        compiler_params=_params(("arbitrary",)),
        name="sweep_ffn",
    )(page_table, q, cache_kt, x, win, wout, g, b)


def _moba_sample_kernel(pt_ref, idx_ref, q_ref, kn_ref, vn_ref, lg_ref, cv_ref, o_ref, vbuf, sem, *, layer, ppb):
    b = pl.program_id(0)
    slot = b % 2
    rows = cv_ref.shape[-1]
    slabs = [(h, r, pg) for h in range(N_HEADS) for r in range(TOP_K) for pg in range(ppb)]

    def slab_copy(seq, buf, h, r, pg):
        blk = idx_ref[seq, r * N_HEADS + h]
        phys = pt_ref[seq, blk * ppb + pg]
        return pltpu.make_async_copy(cv_ref.at[layer, phys, h], vbuf.at[buf, h, r * ppb + pg], sem.at[buf])

    @pl.when(b == 0)
    def _():
        for s in slabs:
            slab_copy(0, 0, *s).start()

    @pl.when(b + 1 < pl.num_programs(0))
    def _():
        for s in slabs:
            slab_copy(b + 1, 1 - slot, *s).start()

    s_sel = []
    for r in range(TOP_K):
        per_head = []
        for h in range(N_HEADS):
            start = pl.multiple_of(idx_ref[b, r * N_HEADS + h] * BLOCK, BLOCK)
            per_head.append(lg_ref[0, h:h + 1, pl.ds(start, BLOCK)])
        s_sel.append(jnp.concatenate(per_head, axis=0))
    vn = vn_ref[0]
    s_self = jnp.sum(q_ref[0] * kn_ref[0], axis=1, keepdims=True) * (HEAD_DIM ** -0.5)
    m = s_self
    for s in s_sel:
        m = jnp.maximum(m, jnp.max(s, axis=1, keepdims=True))
    p_sel = [jnp.exp(s - m) for s in s_sel]
    p_self = jnp.exp(s_self - m)
    denom = p_self
    for p in p_sel:
        denom = denom + jnp.sum(p, axis=1, keepdims=True)

    for h, r, pg in slabs:
        pltpu.make_async_copy(cv_ref.at[layer, 0, h], vbuf.at[slot, h, r * ppb + pg], sem.at[slot]).wait()

    for h in range(N_HEADS):
        acc = None
        for r in range(TOP_K):
            for pg in range(ppb):
                term = vbuf[slot, h, r * ppb + pg] * p_sel[r][h:h + 1, pg * rows:(pg + 1) * rows]
                acc = term if acc is None else acc + term
        num = jnp.sum(acc.T, axis=0, keepdims=True) + p_self[h:h + 1, :] * vn[h:h + 1, :]
        o_ref[0, h:h + 1, :] = num / denom[h:h + 1, :]


def _moba_sample(page_table, idx, q, k_new, v_new, logits, cache_vt, layer):
    db = q.shape[0]
    _, _, nh, hd, rows = cache_vt.shape
    ppb = BLOCK // rows
    seq_spec = pl.BlockSpec((1, nh, hd), lambda b, pt, ix: (b, 0, 0))
    return pl.pallas_call(
        functools.partial(_moba_sample_kernel, layer=layer, ppb=ppb),
        grid_spec=pltpu.PrefetchScalarGridSpec(
            num_scalar_prefetch=2,
            grid=(db,),
            in_specs=[seq_spec, seq_spec, seq_spec,
                      pl.BlockSpec((1, nh, logits.shape[2]), lambda b, pt, ix: (b, 0, 0)),
                      pl.BlockSpec(memory_space=pl.ANY)],
            out_specs=seq_spec,
            scratch_shapes=[pltpu.VMEM((2, nh, TOP_K * ppb, hd, rows), F32), pltpu.SemaphoreType.DMA((2,))],
        ),
        out_shape=jax.ShapeDtypeStruct((db, nh, hd), F32),
        compiler_params=_params(("arbitrary",)),
        name="moba_sample",
    )(page_table, idx, q, k_new, v_new, logits, cache_vt)


def _merge_ffn_kernel(x_ref, ya_ref, yb_ref, hg_ref, bg_ref, wa_ref, wb_ref, wo_ref, g2_ref, b2_ref,
                      win_ref, wout_ref, g3_ref, b3_ref, o_ref, *, alpha):
    x = x_ref[...]
    dm = x.shape[1]
    gates = jax.nn.sigmoid(hg_ref[...] + bg_ref[...])
    mix_in = gates[:, :dm] * _dot(ya_ref[...], wa_ref[...]) + gates[:, dm:] * _dot(yb_ref[...], wb_ref[...])
    x2 = _ln(alpha * x + _dot(mix_in.astype(BF16), wo_ref[...]), g2_ref[...], b2_ref[...])
    h = _swiglu(x2.astype(BF16), win_ref, wout_ref)
    o_ref[...] = _ln(alpha * x2 + 0.5 * h, g3_ref[...], b3_ref[...])


def _merge_ffn(x, ya, yb, hg, bg, wa, wb, wo, g2, b2, win, wout, g3, b3, *, alpha, tm):
    n, d = x.shape
    row = lambda i: (i, 0)
    tiles = [x, ya, yb, hg]
    consts = [bg, wa, wb, wo, g2, b2, win, wout, g3, b3]
    return pl.pallas_call(
        functools.partial(_merge_ffn_kernel, alpha=alpha),
        grid=(n // tm,),
        in_specs=[pl.BlockSpec((tm, a.shape[1]), row) for a in tiles] + [_resident(a.shape) for a in consts],
        out_specs=pl.BlockSpec((tm, d), row),
        out_shape=jax.ShapeDtypeStruct((n, d), F32),
        compiler_params=_params(("parallel",)),
        name="merge_ffn",
    )(*tiles, *consts)


def _rope_tables(pos):
    half = HEAD_DIM // 2
    inv = ROPE_THETA ** (-jnp.arange(half, dtype=F32) / half)
    ang = pos.astype(F32)[:, None] * inv[None, :]
    cos, sin = jnp.cos(ang), jnp.sin(ang)
    return (jnp.tile(jnp.concatenate([cos, cos], -1), (1, N_HEADS)),
            jnp.tile(jnp.concatenate([-sin, sin], -1), (1, N_HEADS)))


def kernel(x_prompt, x_sample, cache_k, cache_v, page_table, w_ffn1_in, w_ffn1_out, ln1_g, ln1_b, w_in, w_s, b_s,
           ln_v_g, ln_v_b, b_gate, w_proj_a, w_proj_b, w_o, ln2_g, ln2_b, w_ffn2_in, w_ffn2_out, ln3_g, ln3_b):
    batch, seq, d_model = x_prompt.shape
    db, dec_t, _ = x_sample.shape
    depth = w_in.shape[0]
    assert dec_t == 1, "sample group decodes one token per sequence"
    assert seq % BLOCK == 0 and N_GROUPS * 2 * (LANES // 2) == 2 * GMLP_WIDTH
    assert cache_k.shape[3:] == (N_HEADS, HEAD_DIM) and BLOCK % cache_k.shape[2] == 0
    past_len = page_table.shape[1] * cache_k.shape[2]
    alpha = (2 * depth) ** 0.25
    cos_p, sin_p = _rope_tables(jnp.arange(seq))
    cos_s, sin_s = _rope_tables(past_len + jnp.arange(dec_t))
    row = lambda a: a.reshape(1, -1)
    cache_kt = jnp.transpose(cache_k, (0, 1, 3, 4, 2))
    cache_vt = jnp.transpose(cache_v, (0, 1, 3, 4, 2))

    xp = x_prompt.reshape(batch * seq, d_model)
    xs = x_sample.reshape(db * dec_t, d_model)
    k_p, v_p, k_s, v_s, cv_s = [], [], [], [], []
    for l in range(depth):
        wf1i, wf1o = w_ffn1_in[l].astype(BF16), w_ffn1_out[l].astype(BF16)
        wf2i, wf2o = w_ffn2_in[l].astype(BF16), w_ffn2_out[l].astype(BF16)
        wi, wa, wb, wo = (w[l].astype(BF16) for w in (w_in, w_proj_a, w_proj_b, w_o))
        g1, b1, g2, b2, g3, b3 = (row(a[l]) for a in (ln1_g, ln1_b, ln2_g, ln2_b, ln3_g, ln3_b))
        lvg, lvb, bg = row(ln_v_g[l]), row(ln_v_b[l]), row(b_gate[l])
        gwid = GMLP_WIDTH // N_GROUPS
        bias_p = jnp.repeat(b_s[l].T, gwid, axis=1)
        scale_s = row(jnp.repeat(w_s[l, :, 0, 0], gwid))
        bias_s = row(jnp.repeat(b_s[l, :, 0], gwid))

        x1s = _ffn_ln(xs, wf1i, wf1o, g1, b1, alpha=alpha, tm=db)
        ya_s, q, k, v, hg_s, vg = _mixer(x1s, wi, lvg, lvb, cos_s, sin_s, scale_s, bias_s, spatial=False, tm=db)
        qh, kh, vh = (a.reshape(db, N_HEADS, HEAD_DIM) for a in (q, k, v))
        logits, idx, x1 = _sweep_ffn(cache_kt, page_table, q.reshape(db, 1, ATT_WIDTH), l, xp, wf1i, wf1o, g1, b1,
                                     alpha=alpha)
        k_s.append(k.reshape(db, dec_t, N_HEADS, HEAD_DIM))
        v_s.append(v.reshape(db, dec_t, N_HEADS, HEAD_DIM))
        cv_s.append(vg.reshape(db, dec_t, GMLP_WIDTH))

        ya, q, kt, vt, hg, kb, vb, ksum = _mixer(x1, wi, lvg, lvb, cos_p, sin_p, w_s[l], bias_p,
                                                 spatial=True, tm=2 * BLOCK)
        yb = _moba_prompt(q, kb, vb, ksum, batch=batch, seq=seq)
        xp = _merge_ffn(x1, ya, yb, hg, bg, wa, wb, wo, g2, b2, wf2i, wf2o, g3, b3, alpha=alpha, tm=512)
        k_p.append(kt.reshape(batch, N_HEADS, HEAD_DIM, seq).transpose(0, 3, 1, 2))
        v_p.append(vt.reshape(batch, N_HEADS, HEAD_DIM, seq).transpose(0, 3, 1, 2))

        yb = _moba_sample(page_table, idx.reshape(db, TOP_K * N_HEADS), qh, kh, vh, logits, cache_vt, l)
        yb = yb.reshape(db, ATT_WIDTH).astype(BF16)
        xs = _merge_ffn(x1s, ya_s, yb, hg_s, bg, wa, wb, wo, g2, b2, wf2i, wf2o, g3, b3, alpha=alpha, tm=db)
    return (xp.reshape(batch, seq, d_model), xs.reshape(db, dec_t, d_model),
            jnp.stack(k_p), jnp.stack(v_p), jnp.stack(k_s), jnp.stack(v_s), jnp.stack(cv_s))
```

```python
import functools

import jax
import jax.numpy as jnp
from jax import lax
from jax.experimental import pallas as pl
from jax.experimental.pallas import tpu as pltpu

F32 = jnp.float32
BF16 = jnp.bfloat16

N_HEADS = 8
HEAD_DIM = 64
ATT_WIDTH = N_HEADS * HEAD_DIM
BLOCK = 256
TOP_K = 3
ROPE_THETA = 10000.0
CHUNK = 128
N_GROUPS = 8
GMLP_WIDTH = 512
LN_EPS = 1e-5
NEG_INF = -1e30
LANES = 128
FF_CHUNK = 256
VMEM_LIMIT = 56 * 1024 * 1024


def _resident(shape):
    zeros = (0,) * len(shape)
    return pl.BlockSpec(shape, lambda *_: zeros, pipeline_mode=pl.Buffered(1))


def _params(sem):
    return pltpu.CompilerParams(dimension_semantics=sem, vmem_limit_bytes=VMEM_LIMIT)


def _ln(y, g, b):
    mu = jnp.mean(y, axis=-1, keepdims=True)
    d = y - mu
    var = jnp.mean(d * d, axis=-1, keepdims=True)
    return d * lax.rsqrt(var + LN_EPS) * g + b


def _dot(a, b):
    return jnp.dot(a, b, preferred_element_type=F32)


def _dot_nt(a, b):
    return lax.dot_general(a, b, (((1,), (1,)), ((), ())), preferred_element_type=F32)


def _swiglu(xb, win_ref, wout_ref, chunks=None, acc=None):
    d_ff = wout_ref.shape[0]
    for c in (range(d_ff // FF_CHUNK) if chunks is None else chunks):
        lo, hi = c * FF_CHUNK, (c + 1) * FF_CHUNK
        a = _dot(xb, win_ref[:, lo:hi])
        g = _dot(xb, win_ref[:, d_ff + lo:d_ff + hi])
        act = (g * jax.nn.sigmoid(g) * a).astype(BF16)
        part = _dot(act, wout_ref[lo:hi, :])
        acc = part if acc is None else acc + part
    return acc


def _ffn_ln_kernel(x_ref, win_ref, wout_ref, g_ref, b_ref, o_ref, *, alpha):
    x = x_ref[...]
    h = _swiglu(x.astype(BF16), win_ref, wout_ref)
    o_ref[...] = _ln(alpha * x + 0.5 * h, g_ref[...], b_ref[...])


def _ffn_ln(x, win, wout, g, b, *, alpha, tm):
    n, d = x.shape
    row = lambda i: (i, 0)
    return pl.pallas_call(
        functools.partial(_ffn_ln_kernel, alpha=alpha),
        grid=(n // tm,),
        in_specs=[pl.BlockSpec((tm, d), row), _resident(win.shape), _resident(wout.shape),
                  _resident(g.shape), _resident(b.shape)],
        out_specs=pl.BlockSpec((tm, d), row),
        out_shape=jax.ShapeDtypeStruct((n, d), F32),
        compiler_params=_params(("parallel",)),
        name="ffn_ln",
    )(x, win, wout, g, b)


def _rope(x, cos, sin_signed):
    lane = lax.broadcasted_iota(jnp.int32, x.shape, 1)
    first = (lane % HEAD_DIM) < (HEAD_DIM // 2)
    half = HEAD_DIM // 2
    partner = jnp.where(first, pltpu.roll(x, x.shape[1] - half, 1), pltpu.roll(x, half, 1))
    return x * cos + partner * sin_signed


def _mixer_kernel(x_ref, w_ref, lng_ref, lnb_ref, cos_ref, sin_ref, ws_ref, bs_ref, *out_refs, spatial):
    if spatial:
        ya_ref, q_ref, k_ref, v_ref, hg_ref, kb_ref, vb_ref, ksum_ref = out_refs
    else:
        ya_ref, q_ref, k_ref, v_ref, hg_ref, vg_ref = out_refs
    xb = x_ref[...].astype(BF16)
    gw, aw = GMLP_WIDTH, ATT_WIDTH

    def proj(lo, hi):
        return _dot(xb, w_ref[:, lo:hi])

    u = jax.nn.gelu(proj(0, gw))
    vg = _ln(jax.nn.gelu(proj(gw, 2 * gw)), lng_ref[...], lnb_ref[...])
    if spatial:
        tm = x_ref.shape[0]
        r = lax.broadcasted_iota(jnp.int32, (CHUNK, CHUNK), 0)
        c = lax.broadcasted_iota(jnp.int32, (CHUNK, CHUNK), 1)
        tril = r >= c
        lane = lax.broadcasted_iota(jnp.int32, (CHUNK, LANES), 1)
        lo_half = lane < (LANES // 2)
        for p in range(N_GROUPS // 2):
            wcat = jnp.concatenate(
                [jnp.where(tril, ws_ref[2 * p], 0.0), jnp.where(tril, ws_ref[2 * p + 1], 0.0)],
                axis=1).astype(BF16)
            cols = slice(p * LANES, (p + 1) * LANES)
            for ch in range(tm // CHUNK):
                rows = slice(ch * CHUNK, (ch + 1) * CHUNK)
                vp = vg[rows, cols]
                vstack = jnp.concatenate(
                    [jnp.where(lo_half, vp, 0.0), jnp.where(lo_half, 0.0, vp)], axis=0).astype(BF16)
                mixed = _dot(wcat, vstack) + bs_ref[:, cols]
                ya_ref[rows, cols] = (u[rows, cols] * mixed).astype(ya_ref.dtype)
    else:
        ya_ref[...] = (u * (vg * ws_ref[...] + bs_ref[...])).astype(ya_ref.dtype)
        vg_ref[...] = vg
    cos, sin = cos_ref[...], sin_ref[...]
    q_ref[...] = _rope(proj(2 * gw, 2 * gw + aw), cos, sin)
    k = _rope(proj(2 * gw + aw, 2 * gw + 2 * aw), cos, sin)
    v = proj(2 * gw + 2 * aw, 2 * gw + 3 * aw)
    if spatial:
        k_ref[0] = k.T
        v_ref[0] = v.T
        kb_ref[...] = k.astype(BF16)
        vb_ref[...] = v.astype(BF16)
        for j in range(k.shape[0] // BLOCK):
            ksum_ref[j] = jnp.sum(k[j * BLOCK:(j + 1) * BLOCK], axis=0, keepdims=True)
    else:
        k_ref[...] = k
        v_ref[...] = v
    hg_ref[...] = proj(2 * gw + 3 * aw, w_ref.shape[1])


def _mixer(x, w, lng, lnb, cos, sin, ws, bs, *, spatial, tm):
    n, d = x.shape
    gate_w = w.shape[1] - 2 * GMLP_WIDTH - 3 * ATT_WIDTH
    row = lambda i: (i, 0)
    if spatial:
        n_tab = cos.shape[0] // tm
        tab = pl.BlockSpec((tm, ATT_WIDTH), lambda i: (i % n_tab, 0))
    else:
        tab = _resident(cos.shape)
    outs = [(GMLP_WIDTH, BF16), (ATT_WIDTH, F32), (ATT_WIDTH, F32), (ATT_WIDTH, F32), (gate_w, F32)]
    outs += [(ATT_WIDTH, BF16), (ATT_WIDTH, BF16)] if spatial else [(GMLP_WIDTH, F32)]
    out_specs = [pl.BlockSpec((tm, wd), row) for wd, _ in outs]
    out_shape = [jax.ShapeDtypeStruct((n, wd), dt) for wd, dt in outs]
    if spatial:
        assert tm % BLOCK == 0 and cos.shape[0] % tm == 0, "mixer tiles hold whole MoBA key blocks of one sequence"
        out_specs.append(pl.BlockSpec((tm // BLOCK, 1, ATT_WIDTH), lambda i: (i, 0, 0)))
        out_shape.append(jax.ShapeDtypeStruct((n // BLOCK, 1, ATT_WIDTH), F32))
        for kv in (2, 3):
            out_specs[kv] = pl.BlockSpec((1, ATT_WIDTH, tm), lambda i: (i // n_tab, 0, i % n_tab))
            out_shape[kv] = jax.ShapeDtypeStruct((n // cos.shape[0], ATT_WIDTH, cos.shape[0]), F32)
    return pl.pallas_call(
        functools.partial(_mixer_kernel, spatial=spatial),
        grid=(n // tm,),
        in_specs=[pl.BlockSpec((tm, d), row), _resident(w.shape), _resident(lng.shape), _resident(lnb.shape),
                  tab, tab, _resident(ws.shape), _resident(bs.shape)],
        out_specs=out_specs,
        out_shape=out_shape,
        compiler_params=_params(("parallel",)),
        name="mixer_prompt" if spatial else "mixer_sample",
    )(x, w, lng, lnb, cos, sin, ws, bs)


def _moba_prompt_kernel(q_ref, k_ref, v_ref, ksum_ref, o_ref, kme_ref, sel_ref, lhs_ref, s_ref, mrun_ref, acc_ref,
                        *, nb):
    qb = pl.program_id(1)
    nh = N_HEADS
    width = q_ref.shape[1]

    @pl.when(qb == 0)
    def _():
        km = ksum_ref[...].reshape(nb, width) * (1.0 / BLOCK)
        head_of_lane = lax.broadcasted_iota(jnp.int32, (nb, width), 1) // HEAD_DIM
        kme_ref[...] = jnp.concatenate(
            [jnp.where(head_of_lane == h, km, 0.0) for h in range(nh) for _ in range(2)], axis=0)
        lane_b = lax.broadcasted_iota(jnp.int32, (BLOCK, LANES), 1)
        for j in range(nb):
            sel_ref[j] = jnp.where(lane_b == j, 1.0, 0.0).astype(BF16)

    q = q_ref[...]
    s = lax.dot_general(q, kme_ref[...], (((1,), (1,)), ((), ())),
                        precision=lax.Precision.HIGHEST, preferred_element_type=F32)
    lane = lax.broadcasted_iota(jnp.int32, s.shape, 1)
    jb = lane % nb
    s = jnp.where(jb < qb, s, NEG_INF)
    rank = jnp.zeros(s.shape, jnp.int32)
    for r in range(1, nb):
        sr = pltpu.roll(s, r, 1)
        jr = (jb + (nb - r)) % nb
        ahead = (sr > s) | ((sr == s) & (jr < jb))
        rank = rank + ahead.astype(jnp.int32)
    bias = jnp.where((rank < min(TOP_K, nb)) & (jb < qb), 0.0, NEG_INF)

    lo_half = lax.broadcasted_iota(jnp.int32, (BLOCK, LANES), 1) < HEAD_DIM
    rr = lax.broadcasted_iota(jnp.int32, (BLOCK, BLOCK), 0)
    cc = lax.broadcasted_iota(jnp.int32, (BLOCK, BLOCK), 1)
    causal = rr >= cc
    zeros_b = jnp.zeros((BLOCK, LANES), BF16)
    ones_b = jnp.ones((BLOCK, LANES), BF16)

    def rows_of(j):
        return pl.ds(pl.multiple_of(j * BLOCK, BLOCK), BLOCK)

    def half_max(x):
        return jnp.maximum(x[:, :LANES], x[:, LANES:])

    pair_cols = [slice(hp * LANES, (hp + 1) * LANES) for hp in range(nh // 2)]

    for h in range(nh):
        qp = q[:, pair_cols[h // 2]] * (HEAD_DIM ** -0.5)
        qh = jnp.where(lo_half, qp, 0.0) if h % 2 == 0 else jnp.where(lo_half, 0.0, qp)
        bh = pltpu.roll(bias, (LANES - (2 * nb * h + nb)) % LANES, 1)
        bh = jnp.where(lane < nb, bh, 0.0)
        lhs_ref[h] = jnp.concatenate([qh, bh], axis=1).astype(BF16)

    for h in range(nh):
        k_own = jnp.concatenate([k_ref[rows_of(qb), pair_cols[h // 2]], zeros_b], axis=1)
        sc = jnp.where(causal, _dot_nt(lhs_ref[h], k_own), NEG_INF)
        s_ref[h, qb] = sc
        mrun_ref[h] = half_max(sc)

    def logits_step(j, carry):
        sel_j = sel_ref[j]
        for h in range(nh):
            k_j = jnp.concatenate([k_ref[rows_of(j), pair_cols[h // 2]], sel_j], axis=1)
            sc = _dot_nt(lhs_ref[h], k_j)
            s_ref[h, j] = sc
            mrun_ref[h] = jnp.maximum(mrun_ref[h], half_max(sc))
        return carry

    lax.fori_loop(0, qb, logits_step, 0)

    for h in range(nh):
        m = jnp.max(mrun_ref[h], axis=1, keepdims=True)
        mrun_ref[h] = jnp.broadcast_to(m, (BLOCK, LANES))
        acc_ref[h] = jnp.zeros(acc_ref.shape[1:], F32)

    def pv_step(j, carry):
        for h in range(nh):
            v_j = jnp.concatenate([v_ref[rows_of(j), pair_cols[h // 2]], ones_b], axis=1)
            m = mrun_ref[h]
            p = jnp.exp(s_ref[h, j] - jnp.concatenate([m, m], axis=1)).astype(BF16)
            acc_ref[h] += _dot(p, v_j)
        return carry

    lax.fori_loop(0, qb + 1, pv_step, 0)
    for hp in range(nh // 2):
        out = [acc_ref[2 * hp + t, :, :LANES] / acc_ref[2 * hp + t, :, LANES:] for t in range(2)]
        o_ref[:, pair_cols[hp]] = jnp.where(lo_half, out[0], out[1]).astype(o_ref.dtype)


def _moba_prompt(q, kb, vb, ksum, *, batch, seq):
    nb = seq // BLOCK
    assert nb * BLOCK == seq and 2 * nb * N_HEADS == LANES, "prompt MoBA sweep expects 8 key blocks of 256"
    width = q.shape[1]
    return pl.pallas_call(
        functools.partial(_moba_prompt_kernel, nb=nb),
        grid=(batch, nb),
        in_specs=[pl.BlockSpec((BLOCK, width), lambda b, i: (b * nb + i, 0)),
                  pl.BlockSpec((seq, width), lambda b, i: (b, 0)),
                  pl.BlockSpec((seq, width), lambda b, i: (b, 0)),
                  pl.BlockSpec((nb, 1, width), lambda b, i: (b, 0, 0))],
        out_specs=pl.BlockSpec((BLOCK, width), lambda b, i: (b * nb + i, 0)),
        out_shape=jax.ShapeDtypeStruct(q.shape, BF16),
        scratch_shapes=[pltpu.VMEM((2 * nb * N_HEADS, width), F32),
                        pltpu.VMEM((nb, BLOCK, LANES), BF16),
                        pltpu.VMEM((N_HEADS, BLOCK, 2 * LANES), BF16),
                        pltpu.VMEM((N_HEADS, nb, BLOCK, BLOCK), F32),
                        pltpu.VMEM((N_HEADS, BLOCK, LANES), F32),
                        pltpu.VMEM((N_HEADS, BLOCK, 2 * LANES), F32)],
        compiler_params=_params(("parallel", "arbitrary")),
        name="moba_prompt",
    )(q, kb, vb, ksum)


def _column_broadcast(row):
    return jnp.transpose(jnp.broadcast_to(row, (LANES, row.shape[1])))


def _per_head_sum(x):
    return jnp.sum(x.reshape(N_HEADS, HEAD_DIM, x.shape[1]), axis=1)


def _sweep_ffn_kernel(pt_ref, q_ref, ck_ref, x_ref, win_ref, wout_ref, g_ref, b_ref,
                      lg_ref, idx_ref, o_ref, kbuf, sem, part_ref, *, layer, ppb, alpha):
    b = pl.program_id(0)
    half, rows = kbuf.shape[1], kbuf.shape[-1]
    n_chunks = wout_ref.shape[0] // FF_CHUNK
    ffn_part = (range(0, n_chunks // 2), range(n_chunks // 2, n_chunks))
    sub = HEAD_DIM // 8

    def page_copy(seq, part, i):
        return pltpu.make_async_copy(ck_ref.at[layer, pt_ref[seq, part * half + i]], kbuf.at[part, i], sem.at[part])

    @pl.when(b == 0)
    def _():
        for part in range(2):
            for i in range(half):
                page_copy(0, part, i).start()

    qcol = _column_broadcast(q_ref[0] * (HEAD_DIM ** -0.5))
    x = x_ref[...]
    xb = x.astype(BF16)
    ffn = None
    scores = []
    for part in range(2):
        for i in range(half):
            page_copy(b, part, i).wait()
        for h in range(N_HEADS):
            qh = qcol[h * HEAD_DIM:(h + 1) * HEAD_DIM]
            for i in range(half):
                part_ref[i, h] = jnp.sum((kbuf[part, i, h] * qh).reshape(sub, 8, rows), axis=0)
        for j in range(half // ppb):
            tot = None
            for p in range(ppb):
                i = j * ppb + p
                lg = jnp.sum(part_ref[i], axis=1)
                lg_ref[0, :, (part * half + i) * rows:(part * half + i + 1) * rows] = lg
                tot = lg if tot is None else tot + lg
            scores.append(jnp.sum(tot, axis=1, keepdims=True) * (HEAD_DIM ** 0.5 / BLOCK))

        @pl.when(b + 1 < pl.num_programs(0))
        def _():
            for i in range(half):
                page_copy(b + 1, part, i).start()

        ffn = _swiglu(xb, win_ref, wout_ref, ffn_part[part], ffn)

    o_ref[...] = _ln(alpha * x + 0.5 * ffn, g_ref[...], b_ref[...])
    nblk = len(scores)
    picks = []
    for _ in range(TOP_K):
        mx = functools.reduce(jnp.maximum, scores)
        ix = jnp.full(mx.shape, nblk, jnp.int32)
        for j in reversed(range(nblk)):
            ix = jnp.where(scores[j] == mx, j, ix)
        picks.append(ix)
        scores = [jnp.where(ix == j, -jnp.inf, sc) for j, sc in enumerate(scores)]
    idx_ref[0] = jnp.stack(picks, axis=0)


def _sweep_ffn(cache_kt, page_table, q, layer, x, win, wout, g, b, *, alpha):
    _, _, nh, hd, rows = cache_kt.shape
    db, n_pages = page_table.shape
    n, d = x.shape
    ppb = BLOCK // rows
    tm = n // db
    assert n_pages // ppb >= TOP_K and n_pages % (2 * ppb) == 0 and hd % 8 == 0
    assert tm * db == n and tm % 8 == 0, "one prompt row tile per sample sequence"
    const = lambda shape: pl.BlockSpec(shape, lambda i, pt: (0,) * len(shape), pipeline_mode=pl.Buffered(1))
    return pl.pallas_call(
        functools.partial(_sweep_ffn_kernel, layer=layer, ppb=ppb, alpha=alpha),
        grid_spec=pltpu.PrefetchScalarGridSpec(
            num_scalar_prefetch=1,
            grid=(db,),
            in_specs=[pl.BlockSpec((1, 1, nh * hd), lambda i, pt: (i, 0, 0)), pl.BlockSpec(memory_space=pl.ANY),
                      pl.BlockSpec((tm, d), lambda i, pt: (i, 0)),
                      const(win.shape), const(wout.shape), const(g.shape), const(b.shape)],
            out_specs=[pl.BlockSpec((1, nh, n_pages * rows), lambda i, pt: (i, 0, 0)),
                       pl.BlockSpec((1, TOP_K, nh, 1), lambda i, pt: (i, 0, 0, 0)),
                       pl.BlockSpec((tm, d), lambda i, pt: (i, 0))],
            scratch_shapes=[pltpu.VMEM((2, n_pages // 2, nh, hd, rows), F32), pltpu.SemaphoreType.DMA((2,)),
                            pltpu.VMEM((n_pages // 2, nh, 8, rows), F32)],
        ),
        out_shape=[jax.ShapeDtypeStruct((db, nh, n_pages * rows), F32),
                   jax.ShapeDtypeStruct((db, TOP_K, nh, 1), jnp.int32),
                   jax.ShapeDtypeStruct((n, d), F32)],
        compiler_params=_params(("arbitrary",)),
        name="sweep_ffn",
    )(page_table, q, cache_kt, x, win, wout, g, b)


SLAB_SLOTS = 4


def _moba_sample_kernel(pt_ref, idx_ref, q_ref, kn_ref, vn_ref, lg_ref, cv_ref, o_ref, vbuf, sem, *, layer, ppb):
    b = pl.program_id(0)
    n_seq = pl.num_programs(0)
    n_slots = vbuf.shape[0]
    ahead = n_slots - 1
    slot = lax.rem(b, n_slots)
    rows = cv_ref.shape[-1]
    slabs = [(h, r, pg) for h in range(N_HEADS) for r in range(TOP_K) for pg in range(ppb)]

    def slab_copy(seq, buf, h, r, pg):
        blk = idx_ref[seq, r * N_HEADS + h]
        phys = pt_ref[seq, blk * ppb + pg]
        return pltpu.make_async_copy(cv_ref.at[layer, phys, h], vbuf.at[buf, h, r * ppb + pg], sem.at[buf])

    @pl.when(b == 0)
    def _():
        for first in range(ahead):
            @pl.when(first < n_seq)
            def _():
                for s in slabs:
                    slab_copy(first, first, *s).start()

    @pl.when(b + ahead < n_seq)
    def _():
        for s in slabs:
            slab_copy(b + ahead, lax.rem(b + ahead, n_slots), *s).start()

    s_sel = []
    for r in range(TOP_K):
        per_head = []
        for h in range(N_HEADS):
            start = pl.multiple_of(idx_ref[b, r * N_HEADS + h] * BLOCK, BLOCK)
            per_head.append(lg_ref[0, h:h + 1, pl.ds(start, BLOCK)])
        s_sel.append(jnp.concatenate(per_head, axis=0))
    vn = vn_ref[0]
    s_self = jnp.sum(q_ref[0] * kn_ref[0], axis=1, keepdims=True) * (HEAD_DIM ** -0.5)
    m = s_self
    for s in s_sel:
        m = jnp.maximum(m, jnp.max(s, axis=1, keepdims=True))
    p_sel = [jnp.exp(s - m) for s in s_sel]
    p_self = jnp.exp(s_self - m)
    denom = p_self
    for p in p_sel:
        denom = denom + jnp.sum(p, axis=1, keepdims=True)

    for h, r, pg in slabs:
        pltpu.make_async_copy(cv_ref.at[layer, 0, h], vbuf.at[slot, h, r * ppb + pg], sem.at[slot]).wait()

    for h in range(N_HEADS):
        acc = None
        for r in range(TOP_K):
            for pg in range(ppb):
                term = vbuf[slot, h, r * ppb + pg] * p_sel[r][h:h + 1, pg * rows:(pg + 1) * rows]
                acc = term if acc is None else acc + term
        num = jnp.sum(acc.T, axis=0, keepdims=True) + p_self[h:h + 1, :] * vn[h:h + 1, :]
        o_ref[0, h:h + 1, :] = num / denom[h:h + 1, :]


def _moba_sample(page_table, idx, q, k_new, v_new, logits, cache_vt, layer):
    db = q.shape[0]
    _, _, nh, hd, rows = cache_vt.shape
    ppb = BLOCK // rows
    seq_spec = pl.BlockSpec((1, nh, hd), lambda b, pt, ix: (b, 0, 0))
    return pl.pallas_call(
        functools.partial(_moba_sample_kernel, layer=layer, ppb=ppb),
        grid_spec=pltpu.PrefetchScalarGridSpec(
            num_scalar_prefetch=2,
            grid=(db,),
            in_specs=[seq_spec, seq_spec, seq_spec,
                      pl.BlockSpec((1, nh, logits.shape[2]), lambda b, pt, ix: (b, 0, 0)),
                      pl.BlockSpec(memory_space=pl.ANY)],
            out_specs=seq_spec,
            scratch_shapes=[pltpu.VMEM((SLAB_SLOTS, nh, TOP_K * ppb, hd, rows), F32),
                            pltpu.SemaphoreType.DMA((SLAB_SLOTS,))],
        ),
        out_shape=jax.ShapeDtypeStruct((db, nh, hd), F32),
        compiler_params=_params(("arbitrary",)),
        name="moba_sample",
    )(page_table, idx, q, k_new, v_new, logits, cache_vt)


def _merge_ffn_kernel(x_ref, ya_ref, yb_ref, hg_ref, bg_ref, wa_ref, wb_ref, wo_ref, g2_ref, b2_ref,
                      win_ref, wout_ref, g3_ref, b3_ref, o_ref, *, alpha):
    x = x_ref[...]
    dm = x.shape[1]
    gates = jax.nn.sigmoid(hg_ref[...] + bg_ref[...])
    mix_in = gates[:, :dm] * _dot(ya_ref[...], wa_ref[...]) + gates[:, dm:] * _dot(yb_ref[...], wb_ref[...])
    x2 = _ln(alpha * x + _dot(mix_in.astype(BF16), wo_ref[...]), g2_ref[...], b2_ref[...])
    h = _swiglu(x2.astype(BF16), win_ref, wout_ref)
    o_ref[...] = _ln(alpha * x2 + 0.5 * h, g3_ref[...], b3_ref[...])


def _merge_ffn(x, ya, yb, hg, bg, wa, wb, wo, g2, b2, win, wout, g3, b3, *, alpha, tm):
    n, d = x.shape
    row = lambda i: (i, 0)
    tiles = [x, ya, yb, hg]
    consts = [bg, wa, wb, wo, g2, b2, win, wout, g3, b3]
    return pl.pallas_call(
        functools.partial(_merge_ffn_kernel, alpha=alpha),
        grid=(n // tm,),
        in_specs=[pl.BlockSpec((tm, a.shape[1]), row) for a in tiles] + [_resident(a.shape) for a in consts],
        out_specs=pl.BlockSpec((tm, d), row),
        out_shape=jax.ShapeDtypeStruct((n, d), F32),
        compiler_params=_params(("parallel",)),
        name="merge_ffn",
    )(*tiles, *consts)


def _rope_tables(pos):
    half = HEAD_DIM // 2
    inv = ROPE_THETA ** (-jnp.arange(half, dtype=F32) / half)
    ang = pos.astype(F32)[:, None] * inv[None, :]
    cos, sin = jnp.cos(ang), jnp.sin(ang)
    return (jnp.tile(jnp.concatenate([cos, cos], -1), (1, N_HEADS)),
            jnp.tile(jnp.concatenate([-sin, sin], -1), (1, N_HEADS)))


def kernel(x_prompt, x_sample, cache_k, cache_v, page_table, w_ffn1_in, w_ffn1_out, ln1_g, ln1_b, w_in, w_s, b_s,
           ln_v_g, ln_v_b, b_gate, w_proj_a, w_proj_b, w_o, ln2_g, ln2_b, w_ffn2_in, w_ffn2_out, ln3_g, ln3_b):
    batch, seq, d_model = x_prompt.shape
    db, dec_t, _ = x_sample.shape
    depth = w_in.shape[0]
    assert dec_t == 1, "sample group decodes one token per sequence"
    assert seq % BLOCK == 0 and N_GROUPS * 2 * (LANES // 2) == 2 * GMLP_WIDTH
    assert cache_k.shape[3:] == (N_HEADS, HEAD_DIM) and BLOCK % cache_k.shape[2] == 0
    past_len = page_table.shape[1] * cache_k.shape[2]
    alpha = (2 * depth) ** 0.25
    cos_p, sin_p = _rope_tables(jnp.arange(seq))
    cos_s, sin_s = _rope_tables(past_len + jnp.arange(dec_t))
    row = lambda a: a.reshape(1, -1)
    cache_kt = jnp.transpose(cache_k, (0, 1, 3, 4, 2))
    cache_vt = jnp.transpose(cache_v, (0, 1, 3, 4, 2))

    xp = x_prompt.reshape(batch * seq, d_model)
    xs = x_sample.reshape(db * dec_t, d_model)
    k_p, v_p, k_s, v_s, cv_s = [], [], [], [], []
    for l in range(depth):
        wf1i, wf1o = w_ffn1_in[l].astype(BF16), w_ffn1_out[l].astype(BF16)
        wf2i, wf2o = w_ffn2_in[l].astype(BF16), w_ffn2_out[l].astype(BF16)
        wi, wa, wb, wo = (w[l].astype(BF16) for w in (w_in, w_proj_a, w_proj_b, w_o))
        g1, b1, g2, b2, g3, b3 = (row(a[l]) for a in (ln1_g, ln1_b, ln2_g, ln2_b, ln3_g, ln3_b))
        lvg, lvb, bg = row(ln_v_g[l]), row(ln_v_b[l]), row(b_gate[l])
        gwid = GMLP_WIDTH // N_GROUPS
        bias_p = jnp.repeat(b_s[l].T, gwid, axis=1)
        scale_s = row(jnp.repeat(w_s[l, :, 0, 0], gwid))
        bias_s = row(jnp.repeat(b_s[l, :, 0], gwid))

        x1s = _ffn_ln(xs, wf1i, wf1o, g1, b1, alpha=alpha, tm=db)
        ya_s, q, k, v, hg_s, vg = _mixer(x1s, wi, lvg, lvb, cos_s, sin_s, scale_s, bias_s, spatial=False, tm=db)
        qh, kh, vh = (a.reshape(db, N_HEADS, HEAD_DIM) for a in (q, k, v))
        logits, idx, x1 = _sweep_ffn(cache_kt, page_table, q.reshape(db, 1, ATT_WIDTH), l, xp, wf1i, wf1o, g1, b1,
                                     alpha=alpha)
        k_s.append(k.reshape(db, dec_t, N_HEADS, HEAD_DIM))
        v_s.append(v.reshape(db, dec_t, N_HEADS, HEAD_DIM))
        cv_s.append(vg.reshape(db, dec_t, GMLP_WIDTH))

        ya, q, kt, vt, hg, kb, vb, ksum = _mixer(x1, wi, lvg, lvb, cos_p, sin_p, w_s[l], bias_p,
                                                 spatial=True, tm=2 * BLOCK)
        yb = _moba_prompt(q, kb, vb, ksum, batch=batch, seq=seq)
        xp = _merge_ffn(x1, ya, yb, hg, bg, wa, wb, wo, g2, b2, wf2i, wf2o, g3, b3, alpha=alpha, tm=512)
        k_p.append(kt.reshape(batch, N_HEADS, HEAD_DIM, seq).transpose(0, 3, 1, 2))
        v_p.append(vt.reshape(batch, N_HEADS, HEAD_DIM, seq).transpose(0, 3, 1, 2))

        yb = _moba_sample(page_table, idx.reshape(db, TOP_K * N_HEADS), qh, kh, vh, logits, cache_vt, l)
        yb = yb.reshape(db, ATT_WIDTH).astype(BF16)
        xs = _merge_ffn(x1s, ya_s, yb, hg_s, bg, wa, wb, wo, g2, b2, wf2i, wf2o, g3, b3, alpha=alpha, tm=db)
    return (xp.reshape(batch, seq, d_model), xs.reshape(db, dec_t, d_model),
            jnp.stack(k_p), jnp.stack(v_p), jnp.stack(k_s), jnp.stack(v_s), jnp.stack(cv_s))
```

```python
import functools

import jax
import jax.numpy as jnp
from jax import lax
from jax.experimental import pallas as pl
from jax.experimental.pallas import tpu as pltpu

F32 = jnp.float32
BF16 = jnp.bfloat16

N_HEADS = 8
HEAD_DIM = 64
ATT_WIDTH = N_HEADS * HEAD_DIM
BLOCK = 256
TOP_K = 3
ROPE_THETA = 10000.0
CHUNK = 128
N_GROUPS = 8
GMLP_WIDTH = 512
LN_EPS = 1e-5
NEG_INF = -1e30
LANES = 128
FF_CHUNK = 256
VMEM_LIMIT = 56 * 1024 * 1024


def _resident(shape):
    zeros = (0,) * len(shape)
    return pl.BlockSpec(shape, lambda *_: zeros, pipeline_mode=pl.Buffered(1))


def _params(sem):
    return pltpu.CompilerParams(dimension_semantics=sem, vmem_limit_bytes=VMEM_LIMIT)


def _ln(y, g, b):
    mu = jnp.mean(y, axis=-1, keepdims=True)
    d = y - mu
    var = jnp.mean(d * d, axis=-1, keepdims=True)
    return d * lax.rsqrt(var + LN_EPS) * g + b


def _dot(a, b):
    return jnp.dot(a, b, preferred_element_type=F32)


def _dot_nt(a, b):
    return lax.dot_general(a, b, (((1,), (1,)), ((), ())), preferred_element_type=F32)


def _swiglu(xb, win_ref, wout_ref, chunks=None, acc=None):
    d_ff = wout_ref.shape[0]
    for c in (range(d_ff // FF_CHUNK) if chunks is None else chunks):
        lo, hi = c * FF_CHUNK, (c + 1) * FF_CHUNK
        a = _dot(xb, win_ref[:, lo:hi])
        g = _dot(xb, win_ref[:, d_ff + lo:d_ff + hi])
        act = (g * jax.nn.sigmoid(g) * a).astype(BF16)
        part = _dot(act, wout_ref[lo:hi, :])
        acc = part if acc is None else acc + part
    return acc


def _ffn_ln_kernel(x_ref, win_ref, wout_ref, g_ref, b_ref, o_ref, *, alpha):
    x = x_ref[...]
    h = _swiglu(x.astype(BF16), win_ref, wout_ref)
    o_ref[...] = _ln(alpha * x + 0.5 * h, g_ref[...], b_ref[...])


def _ffn_ln(x, win, wout, g, b, *, alpha, tm):
    n, d = x.shape
    row = lambda i: (i, 0)
    return pl.pallas_call(
        functools.partial(_ffn_ln_kernel, alpha=alpha),
        grid=(n // tm,),
        in_specs=[pl.BlockSpec((tm, d), row), _resident(win.shape), _resident(wout.shape),
                  _resident(g.shape), _resident(b.shape)],
        out_specs=pl.BlockSpec((tm, d), row),
        out_shape=jax.ShapeDtypeStruct((n, d), F32),
        compiler_params=_params(("parallel",)),
        name="ffn_ln",
    )(x, win, wout, g, b)


def _rope(x, cos, sin_signed):
    lane = lax.broadcasted_iota(jnp.int32, x.shape, 1)
    first = (lane % HEAD_DIM) < (HEAD_DIM // 2)
    half = HEAD_DIM // 2
    partner = jnp.where(first, pltpu.roll(x, x.shape[1] - half, 1), pltpu.roll(x, half, 1))
    return x * cos + partner * sin_signed


def _mixer_kernel(x_ref, w_ref, lng_ref, lnb_ref, cos_ref, sin_ref, ws_ref, bs_ref, *out_refs, spatial):
    if spatial:
        ya_ref, q_ref, k_ref, v_ref, hg_ref, kb_ref, vb_ref, ksum_ref = out_refs
    else:
        ya_ref, q_ref, k_ref, v_ref, hg_ref, vg_ref = out_refs
    xb = x_ref[...].astype(BF16)
    gw, aw = GMLP_WIDTH, ATT_WIDTH

    def proj(lo, hi):
        return _dot(xb, w_ref[:, lo:hi])

    u = jax.nn.gelu(proj(0, gw))
    vg = _ln(jax.nn.gelu(proj(gw, 2 * gw)), lng_ref[...], lnb_ref[...])
    if spatial:
        tm = x_ref.shape[0]
        r = lax.broadcasted_iota(jnp.int32, (CHUNK, CHUNK), 0)
        c = lax.broadcasted_iota(jnp.int32, (CHUNK, CHUNK), 1)
        tril = r >= c
        lane = lax.broadcasted_iota(jnp.int32, (CHUNK, LANES), 1)
        lo_half = lane < (LANES // 2)
        for p in range(N_GROUPS // 2):
            wcat = jnp.concatenate(
                [jnp.where(tril, ws_ref[2 * p], 0.0), jnp.where(tril, ws_ref[2 * p + 1], 0.0)],
                axis=1).astype(BF16)
            cols = slice(p * LANES, (p + 1) * LANES)
            for ch in range(tm // CHUNK):
                rows = slice(ch * CHUNK, (ch + 1) * CHUNK)
                vp = vg[rows, cols]
                vstack = jnp.concatenate(
                    [jnp.where(lo_half, vp, 0.0), jnp.where(lo_half, 0.0, vp)], axis=0).astype(BF16)
                mixed = _dot(wcat, vstack) + bs_ref[:, cols]
                ya_ref[rows, cols] = (u[rows, cols] * mixed).astype(ya_ref.dtype)
    else:
        ya_ref[...] = (u * (vg * ws_ref[...] + bs_ref[...])).astype(ya_ref.dtype)
        vg_ref[...] = vg
    cos, sin = cos_ref[...], sin_ref[...]
    q_ref[...] = _rope(proj(2 * gw, 2 * gw + aw), cos, sin)
    k = _rope(proj(2 * gw + aw, 2 * gw + 2 * aw), cos, sin)
    v = proj(2 * gw + 2 * aw, 2 * gw + 3 * aw)
    if spatial:
        k_ref[0] = k.T
        v_ref[0] = v.T
        kb_ref[...] = k.astype(BF16)
        vb_ref[...] = v.astype(BF16)
        for j in range(k.shape[0] // BLOCK):
            ksum_ref[j] = jnp.sum(k[j * BLOCK:(j + 1) * BLOCK], axis=0, keepdims=True)
    else:
        k_ref[...] = k
        v_ref[...] = v
    hg_ref[...] = proj(2 * gw + 3 * aw, w_ref.shape[1])


def _mixer(x, w, lng, lnb, cos, sin, ws, bs, *, spatial, tm):
    n, d = x.shape
    gate_w = w.shape[1] - 2 * GMLP_WIDTH - 3 * ATT_WIDTH
    row = lambda i: (i, 0)
    if spatial:
        n_tab = cos.shape[0] // tm
        tab = pl.BlockSpec((tm, ATT_WIDTH), lambda i: (i % n_tab, 0))
    else:
        tab = _resident(cos.shape)
    outs = [(GMLP_WIDTH, BF16), (ATT_WIDTH, F32), (ATT_WIDTH, F32), (ATT_WIDTH, F32), (gate_w, F32)]
    outs += [(ATT_WIDTH, BF16), (ATT_WIDTH, BF16)] if spatial else [(GMLP_WIDTH, F32)]
    out_specs = [pl.BlockSpec((tm, wd), row) for wd, _ in outs]
    out_shape = [jax.ShapeDtypeStruct((n, wd), dt) for wd, dt in outs]
    if spatial:
        assert tm % BLOCK == 0 and cos.shape[0] % tm == 0, "mixer tiles hold whole MoBA key blocks of one sequence"
        out_specs.append(pl.BlockSpec((tm // BLOCK, 1, ATT_WIDTH), lambda i: (i, 0, 0)))
        out_shape.append(jax.ShapeDtypeStruct((n // BLOCK, 1, ATT_WIDTH), F32))
        for kv in (2, 3):
            out_specs[kv] = pl.BlockSpec((1, ATT_WIDTH, tm), lambda i: (i // n_tab, 0, i % n_tab))
            out_shape[kv] = jax.ShapeDtypeStruct((n // cos.shape[0], ATT_WIDTH, cos.shape[0]), F32)
    return pl.pallas_call(
        functools.partial(_mixer_kernel, spatial=spatial),
        grid=(n // tm,),
        in_specs=[pl.BlockSpec((tm, d), row), _resident(w.shape), _resident(lng.shape), _resident(lnb.shape),
                  tab, tab, _resident(ws.shape), _resident(bs.shape)],
        out_specs=out_specs,
        out_shape=out_shape,
        compiler_params=_params(("parallel",)),
        name="mixer_prompt" if spatial else "mixer_sample",
    )(x, w, lng, lnb, cos, sin, ws, bs)


def _moba_prompt_kernel(q_ref, k_ref, v_ref, ksum_ref, o_ref, kme_ref, sel_ref, lhs_ref, s_ref, mrun_ref, acc_ref,
                        *, nb):
    qb = pl.program_id(1)
    nh = N_HEADS
    width = q_ref.shape[1]

    @pl.when(qb == 0)
    def _():
        km = ksum_ref[...].reshape(nb, width) * (1.0 / BLOCK)
        head_of_lane = lax.broadcasted_iota(jnp.int32, (nb, width), 1) // HEAD_DIM
        kme_ref[...] = jnp.concatenate(
            [jnp.where(head_of_lane == h, km, 0.0) for h in range(nh) for _ in range(2)], axis=0)
        lane_b = lax.broadcasted_iota(jnp.int32, (BLOCK, LANES), 1)
        for j in range(nb):
            sel_ref[j] = jnp.where(lane_b == j, 1.0, 0.0).astype(BF16)

    q = q_ref[...]
    s = lax.dot_general(q, kme_ref[...], (((1,), (1,)), ((), ())),
                        precision=lax.Precision.HIGHEST, preferred_element_type=F32)
    lane = lax.broadcasted_iota(jnp.int32, s.shape, 1)
    jb = lane % nb
    s = jnp.where(jb < qb, s, NEG_INF)
    rank = jnp.zeros(s.shape, jnp.int32)
    for r in range(1, nb):
        sr = pltpu.roll(s, r, 1)
        jr = (jb + (nb - r)) % nb
        ahead = (sr > s) | ((sr == s) & (jr < jb))
        rank = rank + ahead.astype(jnp.int32)
    bias = jnp.where((rank < min(TOP_K, nb)) & (jb < qb), 0.0, NEG_INF)

    lo_half = lax.broadcasted_iota(jnp.int32, (BLOCK, LANES), 1) < HEAD_DIM
    rr = lax.broadcasted_iota(jnp.int32, (BLOCK, BLOCK), 0)
    cc = lax.broadcasted_iota(jnp.int32, (BLOCK, BLOCK), 1)
    causal = rr >= cc
    zeros_b = jnp.zeros((BLOCK, LANES), BF16)
    ones_b = jnp.ones((BLOCK, LANES), BF16)

    def rows_of(j):
        return pl.ds(pl.multiple_of(j * BLOCK, BLOCK), BLOCK)

    def half_max(x):
        return jnp.maximum(x[:, :LANES], x[:, LANES:])

    pair_cols = [slice(hp * LANES, (hp + 1) * LANES) for hp in range(nh // 2)]

    for h in range(nh):
        qp = q[:, pair_cols[h // 2]] * (HEAD_DIM ** -0.5)
        qh = jnp.where(lo_half, qp, 0.0) if h % 2 == 0 else jnp.where(lo_half, 0.0, qp)
        bh = pltpu.roll(bias, (LANES - (2 * nb * h + nb)) % LANES, 1)
        bh = jnp.where(lane < nb, bh, 0.0)
        lhs_ref[h] = jnp.concatenate([qh, bh], axis=1).astype(BF16)

    for h in range(nh):
        k_own = jnp.concatenate([k_ref[rows_of(qb), pair_cols[h // 2]], zeros_b], axis=1)
        sc = jnp.where(causal, _dot_nt(lhs_ref[h], k_own), NEG_INF)
        s_ref[h, qb] = sc
        mrun_ref[h] = half_max(sc)

    def logits_step(j, carry):
        sel_j = sel_ref[j]
        for h in range(nh):
            k_j = jnp.concatenate([k_ref[rows_of(j), pair_cols[h // 2]], sel_j], axis=1)
            sc = _dot_nt(lhs_ref[h], k_j)
            s_ref[h, j] = sc
            mrun_ref[h] = jnp.maximum(mrun_ref[h], half_max(sc))
        return carry

    lax.fori_loop(0, qb, logits_step, 0)

    for h in range(nh):
        m = jnp.max(mrun_ref[h], axis=1, keepdims=True)
        mrun_ref[h] = jnp.broadcast_to(m, (BLOCK, LANES))
        acc_ref[h] = jnp.zeros(acc_ref.shape[1:], F32)

    def pv_step(j, carry):
        for h in range(nh):
            v_j = jnp.concatenate([v_ref[rows_of(j), pair_cols[h // 2]], ones_b], axis=1)
            m = mrun_ref[h]
            p = jnp.exp(s_ref[h, j] - jnp.concatenate([m, m], axis=1)).astype(BF16)
            acc_ref[h] += _dot(p, v_j)
        return carry

    lax.fori_loop(0, qb + 1, pv_step, 0)
    for hp in range(nh // 2):
        out = [acc_ref[2 * hp + t, :, :LANES] / acc_ref[2 * hp + t, :, LANES:] for t in range(2)]
        o_ref[:, pair_cols[hp]] = jnp.where(lo_half, out[0], out[1]).astype(o_ref.dtype)


def _moba_prompt(q, kb, vb, ksum, *, batch, seq):
    nb = seq // BLOCK
    assert nb * BLOCK == seq and 2 * nb * N_HEADS == LANES, "prompt MoBA sweep expects 8 key blocks of 256"
    width = q.shape[1]
    return pl.pallas_call(
        functools.partial(_moba_prompt_kernel, nb=nb),
        grid=(batch, nb),
        in_specs=[pl.BlockSpec((BLOCK, width), lambda b, i: (b * nb + i, 0)),
                  pl.BlockSpec((seq, width), lambda b, i: (b, 0)),
                  pl.BlockSpec((seq, width), lambda b, i: (b, 0)),
                  pl.BlockSpec((nb, 1, width), lambda b, i: (b, 0, 0))],
        out_specs=pl.BlockSpec((BLOCK, width), lambda b, i: (b * nb + i, 0)),
        out_shape=jax.ShapeDtypeStruct(q.shape, BF16),
        scratch_shapes=[pltpu.VMEM((2 * nb * N_HEADS, width), F32),
                        pltpu.VMEM((nb, BLOCK, LANES), BF16),
                        pltpu.VMEM((N_HEADS, BLOCK, 2 * LANES), BF16),
                        pltpu.VMEM((N_HEADS, nb, BLOCK, BLOCK), F32),
                        pltpu.VMEM((N_HEADS, BLOCK, LANES), F32),
                        pltpu.VMEM((N_HEADS, BLOCK, 2 * LANES), F32)],
        compiler_params=_params(("parallel", "arbitrary")),
        name="moba_prompt",
    )(q, kb, vb, ksum)


def _column_broadcast(row):
    return jnp.transpose(jnp.broadcast_to(row, (LANES, row.shape[1])))


def _per_head_sum(x):
    return jnp.sum(x.reshape(N_HEADS, HEAD_DIM, x.shape[1]), axis=1)


def _sweep_ffn_kernel(pt_ref, q_ref, ck_ref, x_ref, win_ref, wout_ref, g_ref, b_ref,
                      lg_ref, idx_ref, o_ref, kbuf, sem, part_ref, *, layer, ppb, alpha):
    b = pl.program_id(0)
    half, rows = kbuf.shape[1], kbuf.shape[-1]
    n_chunks = wout_ref.shape[0] // FF_CHUNK
    ffn_part = (range(0, n_chunks // 2), range(n_chunks // 2, n_chunks))
    sub = HEAD_DIM // 8

    def page_copy(seq, part, i):
        return pltpu.make_async_copy(ck_ref.at[layer, pt_ref[seq, part * half + i]], kbuf.at[part, i], sem.at[part])

    @pl.when(b == 0)
    def _():
        for part in range(2):
            for i in range(half):
                page_copy(0, part, i).start()

    qcol = _column_broadcast(q_ref[0] * (HEAD_DIM ** -0.5))
    x = x_ref[...]
    xb = x.astype(BF16)
    ffn = None
    scores = []
    for part in range(2):
        for i in range(half):
            page_copy(b, part, i).wait()
        for h in range(N_HEADS):
            qh = qcol[h * HEAD_DIM:(h + 1) * HEAD_DIM]
            for i in range(half):
                part_ref[i, h] = jnp.sum((kbuf[part, i, h] * qh).reshape(sub, 8, rows), axis=0)
        for j in range(half // ppb):
            tot = None
            for p in range(ppb):
                i = j * ppb + p
                lg = jnp.sum(part_ref[i], axis=1)
                lg_ref[0, :, (part * half + i) * rows:(part * half + i + 1) * rows] = lg
                tot = lg if tot is None else tot + lg
            scores.append(jnp.sum(tot, axis=1, keepdims=True) * (HEAD_DIM ** 0.5 / BLOCK))

        @pl.when(b + 1 < pl.num_programs(0))
        def _():
            for i in range(half):
                page_copy(b + 1, part, i).start()

        ffn = _swiglu(xb, win_ref, wout_ref, ffn_part[part], ffn)

    o_ref[...] = _ln(alpha * x + 0.5 * ffn, g_ref[...], b_ref[...])
    nblk = len(scores)
    picks = []
    for _ in range(TOP_K):
        mx = functools.reduce(jnp.maximum, scores)
        ix = jnp.full(mx.shape, nblk, jnp.int32)
        for j in reversed(range(nblk)):
            ix = jnp.where(scores[j] == mx, j, ix)
        picks.append(ix)
        scores = [jnp.where(ix == j, -jnp.inf, sc) for j, sc in enumerate(scores)]
    idx_ref[0] = jnp.stack(picks, axis=0)


def _sweep_ffn(cache_kt, page_table, q, layer, x, win, wout, g, b, *, alpha):
    _, _, nh, hd, rows = cache_kt.shape
    db, n_pages = page_table.shape
    n, d = x.shape
    ppb = BLOCK // rows
    tm = n // db
    assert n_pages // ppb >= TOP_K and n_pages % (2 * ppb) == 0 and hd % 8 == 0
    assert tm * db == n and tm % 8 == 0, "one prompt row tile per sample sequence"
    const = lambda shape: pl.BlockSpec(shape, lambda i, pt: (0,) * len(shape), pipeline_mode=pl.Buffered(1))
    return pl.pallas_call(
        functools.partial(_sweep_ffn_kernel, layer=layer, ppb=ppb, alpha=alpha),
        grid_spec=pltpu.PrefetchScalarGridSpec(
            num_scalar_prefetch=1,
            grid=(db,),
            in_specs=[pl.BlockSpec((1, 1, nh * hd), lambda i, pt: (i, 0, 0)), pl.BlockSpec(memory_space=pl.ANY),
                      pl.BlockSpec((tm, d), lambda i, pt: (i, 0)),
                      const(win.shape), const(wout.shape), const(g.shape), const(b.shape)],
            out_specs=[pl.BlockSpec((1, nh, n_pages * rows), lambda i, pt: (i, 0, 0)),
                       pl.BlockSpec((1, TOP_K, nh, 1), lambda i, pt: (i, 0, 0, 0)),
                       pl.BlockSpec((tm, d), lambda i, pt: (i, 0))],
            scratch_shapes=[pltpu.VMEM((2, n_pages // 2, nh, hd, rows), F32), pltpu.SemaphoreType.DMA((2,)),
                            pltpu.VMEM((n_pages // 2, nh, 8, rows), F32)],
        ),
        out_shape=[jax.ShapeDtypeStruct((db, nh, n_pages * rows), F32),
                   jax.ShapeDtypeStruct((db, TOP_K, nh, 1), jnp.int32),
                   jax.ShapeDtypeStruct((n, d), F32)],
        compiler_params=_params(("arbitrary",)),
        name="sweep_ffn",
    )(page_table, q, cache_kt, x, win, wout, g, b)


SLAB_SLOTS = 3
SEQS_PER_STEP = 2


def _moba_sample_kernel(pt_ref, idx_ref, q_ref, kn_ref, vn_ref, lg_ref, cv_ref, o_ref, vbuf, sem, *, layer, ppb):
    b = pl.program_id(0)
    n_steps = pl.num_programs(0)
    n_slots, group = vbuf.shape[0], vbuf.shape[1]
    ahead = n_slots - 1
    slot = lax.rem(b, n_slots)
    rows = cv_ref.shape[-1]
    slabs = [(g, h, r, pg) for g in range(group) for h in range(N_HEADS) for r in range(TOP_K) for pg in range(ppb)]

    def slab_copy(step, buf, g, h, r, pg):
        seq = step * group + g
        blk = idx_ref[seq, r * N_HEADS + h]
        phys = pt_ref[seq, blk * ppb + pg]
        return pltpu.make_async_copy(cv_ref.at[layer, phys, h], vbuf.at[buf, g, h, r * ppb + pg], sem.at[buf])

    @pl.when(b == 0)
    def _():
        for first in range(ahead):
            @pl.when(first < n_steps)
            def _():
                for s in slabs:
                    slab_copy(first, first, *s).start()

    @pl.when(b + ahead < n_steps)
    def _():
        for s in slabs:
            slab_copy(b + ahead, lax.rem(b + ahead, n_slots), *s).start()

    weights = []
    for g in range(group):
        s_sel = []
        for r in range(TOP_K):
            per_head = []
            for h in range(N_HEADS):
                start = pl.multiple_of(idx_ref[b * group + g, r * N_HEADS + h] * BLOCK, BLOCK)
                per_head.append(lg_ref[g, h:h + 1, pl.ds(start, BLOCK)])
            s_sel.append(jnp.concatenate(per_head, axis=0))
        s_self = jnp.sum(q_ref[g] * kn_ref[g], axis=1, keepdims=True) * (HEAD_DIM ** -0.5)
        m = s_self
        for s in s_sel:
            m = jnp.maximum(m, jnp.max(s, axis=1, keepdims=True))
        p_sel = [jnp.exp(s - m) for s in s_sel]
        p_self = jnp.exp(s_self - m)
        denom = p_self
        for p in p_sel:
            denom = denom + jnp.sum(p, axis=1, keepdims=True)
        weights.append((p_sel, p_self, denom))

    for g, h, r, pg in slabs:
        pltpu.make_async_copy(cv_ref.at[layer, 0, h], vbuf.at[slot, g, h, r * ppb + pg], sem.at[slot]).wait()

    for g in range(group):
        p_sel, p_self, denom = weights[g]
        vn = vn_ref[g]
        for h in range(N_HEADS):
            acc = None
            for r in range(TOP_K):
                for pg in range(ppb):
                    term = vbuf[slot, g, h, r * ppb + pg] * p_sel[r][h:h + 1, pg * rows:(pg + 1) * rows]
                    acc = term if acc is None else acc + term
            num = jnp.sum(acc.T, axis=0, keepdims=True) + p_self[h:h + 1, :] * vn[h:h + 1, :]
            o_ref[g, h:h + 1, :] = num / denom[h:h + 1, :]


def _moba_sample(page_table, idx, q, k_new, v_new, logits, cache_vt, layer):
    db = q.shape[0]
    _, _, nh, hd, rows = cache_vt.shape
    ppb = BLOCK // rows
    group = SEQS_PER_STEP
    assert db % group == 0
    seq_spec = pl.BlockSpec((group, nh, hd), lambda b, pt, ix: (b, 0, 0))
    return pl.pallas_call(
        functools.partial(_moba_sample_kernel, layer=layer, ppb=ppb),
        grid_spec=pltpu.PrefetchScalarGridSpec(
            num_scalar_prefetch=2,
            grid=(db // group,),
            in_specs=[seq_spec, seq_spec, seq_spec,
                      pl.BlockSpec((group, nh, logits.shape[2]), lambda b, pt, ix: (b, 0, 0)),
                      pl.BlockSpec(memory_space=pl.ANY)],
            out_specs=seq_spec,
            scratch_shapes=[pltpu.VMEM((SLAB_SLOTS, group, nh, TOP_K * ppb, hd, rows), F32),
                            pltpu.SemaphoreType.DMA((SLAB_SLOTS,))],
        ),
        out_shape=jax.ShapeDtypeStruct((db, nh, hd), F32),
        compiler_params=_params(("arbitrary",)),
        name="moba_sample",
    )(page_table, idx, q, k_new, v_new, logits, cache_vt)


def _merge_ffn_kernel(x_ref, ya_ref, yb_ref, hg_ref, bg_ref, wa_ref, wb_ref, wo_ref, g2_ref, b2_ref,
                      win_ref, wout_ref, g3_ref, b3_ref, o_ref, *, alpha):
    x = x_ref[...]
    dm = x.shape[1]
    gates = jax.nn.sigmoid(hg_ref[...] + bg_ref[...])
    mix_in = gates[:, :dm] * _dot(ya_ref[...], wa_ref[...]) + gates[:, dm:] * _dot(yb_ref[...], wb_ref[...])
    x2 = _ln(alpha * x + _dot(mix_in.astype(BF16), wo_ref[...]), g2_ref[...], b2_ref[...])
    h = _swiglu(x2.astype(BF16), win_ref, wout_ref)
    o_ref[...] = _ln(alpha * x2 + 0.5 * h, g3_ref[...], b3_ref[...])


def _merge_ffn(x, ya, yb, hg, bg, wa, wb, wo, g2, b2, win, wout, g3, b3, *, alpha, tm):
    n, d = x.shape
    row = lambda i: (i, 0)
    tiles = [x, ya, yb, hg]
    consts = [bg, wa, wb, wo, g2, b2, win, wout, g3, b3]
    return pl.pallas_call(
        functools.partial(_merge_ffn_kernel, alpha=alpha),
        grid=(n // tm,),
        in_specs=[pl.BlockSpec((tm, a.shape[1]), row) for a in tiles] + [_resident(a.shape) for a in consts],
        out_specs=pl.BlockSpec((tm, d), row),
        out_shape=jax.ShapeDtypeStruct((n, d), F32),
        compiler_params=_params(("parallel",)),
        name="merge_ffn",
    )(*tiles, *consts)


def _rope_tables(pos):
    half = HEAD_DIM // 2
    inv = ROPE_THETA ** (-jnp.arange(half, dtype=F32) / half)
    ang = pos.astype(F32)[:, None] * inv[None, :]
    cos, sin = jnp.cos(ang), jnp.sin(ang)
    return (jnp.tile(jnp.concatenate([cos, cos], -1), (1, N_HEADS)),
            jnp.tile(jnp.concatenate([-sin, sin], -1), (1, N_HEADS)))


def kernel(x_prompt, x_sample, cache_k, cache_v, page_table, w_ffn1_in, w_ffn1_out, ln1_g, ln1_b, w_in, w_s, b_s,
           ln_v_g, ln_v_b, b_gate, w_proj_a, w_proj_b, w_o, ln2_g, ln2_b, w_ffn2_in, w_ffn2_out, ln3_g, ln3_b):
    batch, seq, d_model = x_prompt.shape
    db, dec_t, _ = x_sample.shape
    depth = w_in.shape[0]
    assert dec_t == 1, "sample group decodes one token per sequence"
    assert seq % BLOCK == 0 and N_GROUPS * 2 * (LANES // 2) == 2 * GMLP_WIDTH
    assert cache_k.shape[3:] == (N_HEADS, HEAD_DIM) and BLOCK % cache_k.shape[2] == 0
    past_len = page_table.shape[1] * cache_k.shape[2]
    alpha = (2 * depth) ** 0.25
    cos_p, sin_p = _rope_tables(jnp.arange(seq))
    cos_s, sin_s = _rope_tables(past_len + jnp.arange(dec_t))
    row = lambda a: a.reshape(1, -1)
    cache_kt = jnp.transpose(cache_k, (0, 1, 3, 4, 2))
    cache_vt = jnp.transpose(cache_v, (0, 1, 3, 4, 2))

    xp = x_prompt.reshape(batch * seq, d_model)
    xs = x_sample.reshape(db * dec_t, d_model)
    k_p, v_p, k_s, v_s, cv_s = [], [], [], [], []
    for l in range(depth):
        wf1i, wf1o = w_ffn1_in[l].astype(BF16), w_ffn1_out[l].astype(BF16)
        wf2i, wf2o = w_ffn2_in[l].astype(BF16), w_ffn2_out[l].astype(BF16)
        wi, wa, wb, wo = (w[l].astype(BF16) for w in (w_in, w_proj_a, w_proj_b, w_o))
        g1, b1, g2, b2, g3, b3 = (row(a[l]) for a in (ln1_g, ln1_b, ln2_g, ln2_b, ln3_g, ln3_b))
        lvg, lvb, bg = row(ln_v_g[l]), row(ln_v_b[l]), row(b_gate[l])
        gwid = GMLP_WIDTH // N_GROUPS
        bias_p = jnp.repeat(b_s[l].T, gwid, axis=1)
        scale_s = row(jnp.repeat(w_s[l, :, 0, 0], gwid))
        bias_s = row(jnp.repeat(b_s[l, :, 0], gwid))

        x1s = _ffn_ln(xs, wf1i, wf1o, g1, b1, alpha=alpha, tm=db)
        ya_s, q, k, v, hg_s, vg = _mixer(x1s, wi, lvg, lvb, cos_s, sin_s, scale_s, bias_s, spatial=False, tm=db)
        qh, kh, vh = (a.reshape(db, N_HEADS, HEAD_DIM) for a in (q, k, v))
        logits, idx, x1 = _sweep_ffn(cache_kt, page_table, q.reshape(db, 1, ATT_WIDTH), l, xp, wf1i, wf1o, g1, b1,
                                     alpha=alpha)
        k_s.append(k.reshape(db, dec_t, N_HEADS, HEAD_DIM))
        v_s.append(v.reshape(db, dec_t, N_HEADS, HEAD_DIM))
        cv_s.append(vg.reshape(db, dec_t, GMLP_WIDTH))

        ya, q, kt, vt, hg, kb, vb, ksum = _mixer(x1, wi, lvg, lvb, cos_p, sin_p, w_s[l], bias_p,
                                                 spatial=True, tm=2 * BLOCK)
        yb = _moba_prompt(q, kb, vb, ksum, batch=batch, seq=seq)
        xp = _merge_ffn(x1, ya, yb, hg, bg, wa, wb, wo, g2, b2, wf2i, wf2o, g3, b3, alpha=alpha, tm=512)
        k_p.append(kt.reshape(batch, N_HEADS, HEAD_DIM, seq).transpose(0, 3, 1, 2))
        v_p.append(vt.reshape(batch, N_HEADS, HEAD_DIM, seq).transpose(0, 3, 1, 2))

        yb = _moba_sample(page_table, idx.reshape(db, TOP_K * N_HEADS), qh, kh, vh, logits, cache_vt, l)
        yb = yb.reshape(db, ATT_WIDTH).astype(BF16)
        xs = _merge_ffn(x1s, ya_s, yb, hg_s, bg, wa, wb, wo, g2, b2, wf2i, wf2o, g3, b3, alpha=alpha, tm=db)
    return (xp.reshape(batch, seq, d_model), xs.reshape(db, dec_t, d_model),
            jnp.stack(k_p), jnp.stack(v_p), jnp.stack(k_s), jnp.stack(v_s), jnp.stack(cv_s))
```
